```python
import jax, jax.numpy as jnp
from jax import lax
import numpy as np

D_MODEL = 2048
BATCH = 2
SEQ = 4096
DEPTH = 2
DEC_BATCH = 16
DEC_SEQ = 64
PAST_LEN = 1024

CHUNK = 64
N_A_LAYERS = DEPTH // 2
N_B_LAYERS = DEPTH - N_A_LAYERS
CONV_WIDTH = 31
N_HEADS = 16
HEAD_DIM = D_MODEL // N_HEADS
D_FF = ((8 * D_MODEL // 3 + 255) // 256) * 256
Q_BLOCK = 128
EPS = 1e-6
FFN_RES = 0.5

kernel_name = "conformer_conv_stickbreak_yoco_step"


def _rmsnorm(x, g):
    xf = x.astype(jnp.float32)
    y = xf * lax.rsqrt(jnp.mean(xf * xf, axis=-1, keepdims=True) + EPS)
    return (y * g.astype(jnp.float32)).astype(x.dtype)


def _layernorm(x, g, b):
    xf = x.astype(jnp.float32)
    mu = jnp.mean(xf, axis=-1, keepdims=True)
    xc = xf - mu
    y = xc * lax.rsqrt(jnp.mean(xc * xc, axis=-1, keepdims=True) + EPS)
    return (y * g.astype(jnp.float32) + b.astype(jnp.float32)).astype(x.dtype)


def _swiglu(x, w_gate, w_up, w_down):
    return (jax.nn.silu(x @ w_gate) * (x @ w_up)) @ w_down


def _conv_module(h, state, pw1_w, pw1_b, dw_w, dw_b, ln_g, ln_b, pw2_w, pw2_b):
    u = jax.nn.glu(h @ pw1_w + pw1_b, axis=-1)
    ext = jnp.concatenate([state.astype(u.dtype), u], axis=1)
    new_state = ext[:, ext.shape[1] - (CONV_WIDTH - 1):]
    c = lax.conv_general_dilated(
        ext, dw_w[:, None, :].astype(ext.dtype), window_strides=(1,), padding='VALID',
        dimension_numbers=('NWC', 'WIO', 'NWC'), feature_group_count=D_MODEL) + dw_b
    c = jax.nn.silu(_layernorm(c, ln_g, ln_b))
    return c @ pw2_w + pw2_b, new_state


def _stick_breaking(q, k, v, past):
    tq = q.shape[1]
    outs = []
    for i in range(0, tq, Q_BLOCK):
        j = min(i + Q_BLOCK, tq)
        kb = k[:, :past + j]
        vb = v[:, :past + j]
        q_pos = past + jnp.arange(i, j)
        k_pos = jnp.arange(past + j)
        mask = k_pos[None, :] < q_pos[:, None]
        z = jnp.einsum('bqhd,bkhd->bhqk', q[:, i:j], kb).astype(jnp.float32) * (HEAD_DIM ** -0.5)
        log_beta = jax.nn.log_sigmoid(z)
        log_keep = jnp.where(mask, log_beta - z, 0.0)
        stick = lax.cumsum(log_keep, axis=3, reverse=True) - log_keep
        a = jnp.where(mask, jnp.exp(log_beta + stick), 0.0)
        outs.append(jnp.einsum('bhqk,bkhd->bqhd', a.astype(vb.dtype), vb))
    return jnp.concatenate(outs, axis=1)


def _trunk(x, conv_states, cache_k, cache_v, weights):
    (ffn1_norm, ffn1_w_gate, ffn1_w_up, ffn1_w_down, mix_norm,
     ffn2_norm, ffn2_w_gate, ffn2_w_up, ffn2_w_down,
     conv_pw1_w, conv_pw1_b, conv_dw_w, conv_dw_b, conv_ln_g, conv_ln_b,
     conv_pw2_w, conv_pw2_b, kv_norm, w_kv, attn_wq, attn_wo, final_norm) = weights
    b_sz, t_len, _ = x.shape
    hk = N_HEADS * HEAD_DIM
    past = 0 if cache_k is None else cache_k.shape[1]
    new_conv = []
    k_new = v_new = k_all = v_all = None
    for l in range(DEPTH):
        if l == N_A_LAYERS:
            kv = _rmsnorm(x, kv_norm) @ w_kv
            k_new = kv[..., :hk].reshape(b_sz, t_len, N_HEADS, HEAD_DIM)
            v_new = kv[..., hk:].reshape(b_sz, t_len, N_HEADS, HEAD_DIM)
            if cache_k is None:
                k_all, v_all = k_new, v_new
            else:
                k_all = jnp.concatenate([cache_k.astype(k_new.dtype), k_new], axis=1)
                v_all = jnp.concatenate([cache_v.astype(v_new.dtype), v_new], axis=1)
        x = x + FFN_RES * _swiglu(_rmsnorm(x, ffn1_norm[l]), ffn1_w_gate[l], ffn1_w_up[l], ffn1_w_down[l])
        h = _rmsnorm(x, mix_norm[l])
        if l < N_A_LAYERS:
            y, st = _conv_module(h, conv_states[l], conv_pw1_w[l], conv_pw1_b[l], conv_dw_w[l],
                                 conv_dw_b[l], conv_ln_g[l], conv_ln_b[l], conv_pw2_w[l], conv_pw2_b[l])
            new_conv.append(st)
        else:
            bl = l - N_A_LAYERS
            q = (h @ attn_wq[bl]).reshape(b_sz, t_len, N_HEADS, HEAD_DIM)
            o = _stick_breaking(q, k_all, v_all, past)
            y = o.reshape(b_sz, t_len, hk) @ attn_wo[bl]
        x = x + y
        x = x + FFN_RES * _swiglu(_rmsnorm(x, ffn2_norm[l]), ffn2_w_gate[l], ffn2_w_up[l], ffn2_w_down[l])
    return _rmsnorm(x, final_norm), jnp.stack(new_conv, axis=0), k_new, v_new


def setup_inputs(seed: int = 0) -> dict:
    key = jax.random.key(seed)
    ks = jax.random.split(key, 32)
    f32 = jnp.float32
    hk = N_HEADS * HEAD_DIM

    def w(k, shape, fan_in):
        return jax.random.normal(k, shape, f32) * (fan_in ** -0.5)

    def gain(k, shape):
        return 1.0 + 0.02 * jax.random.normal(k, shape, f32)

    def bias(k, shape):
        return 0.01 * jax.random.normal(k, shape, f32)

    return {
        "x_prompt": jax.random.normal(ks[0], (BATCH, SEQ, D_MODEL), f32),
        "x_sample": jax.random.normal(ks[1], (DEC_BATCH, DEC_SEQ, D_MODEL), f32),
        "state_conv": 0.5 * jax.random.normal(ks[2], (N_A_LAYERS, DEC_BATCH, CONV_WIDTH - 1, D_MODEL), f32),
        "cache_k": jax.random.normal(ks[3], (DEC_BATCH, PAST_LEN, N_HEADS, HEAD_DIM), f32),
        "cache_v": jax.random.normal(ks[4], (DEC_BATCH, PAST_LEN, N_HEADS, HEAD_DIM), f32),
        "ffn1_norm": gain(ks[5], (DEPTH, D_MODEL)),
        "ffn1_w_gate": w(ks[6], (DEPTH, D_MODEL, D_FF), D_MODEL),
        "ffn1_w_up": w(ks[7], (DEPTH, D_MODEL, D_FF), D_MODEL),
        "ffn1_w_down": w(ks[8], (DEPTH, D_FF, D_MODEL), D_FF),
        "mix_norm": gain(ks[9], (DEPTH, D_MODEL)),
        "ffn2_norm": gain(ks[10], (DEPTH, D_MODEL)),
        "ffn2_w_gate": w(ks[11], (DEPTH, D_MODEL, D_FF), D_MODEL),
        "ffn2_w_up": w(ks[12], (DEPTH, D_MODEL, D_FF), D_MODEL),
        "ffn2_w_down": w(ks[13], (DEPTH, D_FF, D_MODEL), D_FF),
        "conv_pw1_w": w(ks[14], (N_A_LAYERS, D_MODEL, 2 * D_MODEL), D_MODEL),
        "conv_pw1_b": bias(ks[15], (N_A_LAYERS, 2 * D_MODEL)),
        "conv_dw_w": w(ks[16], (N_A_LAYERS, CONV_WIDTH, D_MODEL), CONV_WIDTH),
        "conv_dw_b": bias(ks[17], (N_A_LAYERS, D_MODEL)),
        "conv_ln_g": gain(ks[18], (N_A_LAYERS, D_MODEL)),
        "conv_ln_b": bias(ks[19], (N_A_LAYERS, D_MODEL)),
        "conv_pw2_w": w(ks[20], (N_A_LAYERS, D_MODEL, D_MODEL), D_MODEL),
        "conv_pw2_b": bias(ks[21], (N_A_LAYERS, D_MODEL)),
        "kv_norm": gain(ks[22], (D_MODEL,)),
        "w_kv": w(ks[23], (D_MODEL, 2 * hk), D_MODEL),
        "attn_wq": w(ks[24], (N_B_LAYERS, D_MODEL, hk), D_MODEL),
        "attn_wo": w(ks[25], (N_B_LAYERS, hk, D_MODEL), hk),
        "final_norm": gain(ks[26], (D_MODEL,)),
    }


def reference(x_prompt, x_sample, state_conv, cache_k, cache_v,
              ffn1_norm, ffn1_w_gate, ffn1_w_up, ffn1_w_down, mix_norm,
              ffn2_norm, ffn2_w_gate, ffn2_w_up, ffn2_w_down,
              conv_pw1_w, conv_pw1_b, conv_dw_w, conv_dw_b, conv_ln_g, conv_ln_b,
              conv_pw2_w, conv_pw2_b, kv_norm, w_kv, attn_wq, attn_wo, final_norm):
    weights = (ffn1_norm, ffn1_w_gate, ffn1_w_up, ffn1_w_down, mix_norm,
               ffn2_norm, ffn2_w_gate, ffn2_w_up, ffn2_w_down,
               conv_pw1_w, conv_pw1_b, conv_dw_w, conv_dw_b, conv_ln_g, conv_ln_b,
               conv_pw2_w, conv_pw2_b, kv_norm, w_kv, attn_wq, attn_wo, final_norm)
    zero_conv = jnp.zeros((N_A_LAYERS, x_prompt.shape[0], CONV_WIDTH - 1, D_MODEL), x_prompt.dtype)
    y_prompt, state_conv_prompt, k_prompt, v_prompt = _trunk(x_prompt, zero_conv, None, None, weights)
    y_sample, state_conv_sample, k_sample, v_sample = _trunk(x_sample, state_conv, cache_k, cache_v, weights)
    return (y_prompt, y_sample, state_conv_prompt, k_prompt, v_prompt,
            state_conv_sample, k_sample, v_sample)
```

```python
import functools

import jax
import jax.numpy as jnp
from jax import lax
from jax.experimental import pallas as pl
from jax.experimental.pallas import tpu as pltpu

D_MODEL = 2048
N_HEADS = 16
HEAD_DIM = 128
CONV_WIDTH = 31
EPS = 1e-6
FFN_RES = 0.5
SCALE = HEAD_DIM ** -0.5

HALO = CONV_WIDTH - 1
HALO_PAD = 32

VMEM_LIMIT = 56 * 1024 * 1024

F32 = jnp.float32
BF16 = jnp.bfloat16


def _params(*sem):
    return pltpu.CompilerParams(dimension_semantics=sem, vmem_limit_bytes=VMEM_LIMIT)


def _rms(x, g):
    return x * lax.rsqrt(jnp.mean(x * x, axis=-1, keepdims=True) + EPS) * g


def _ffn_kernel(x_ref, g_ref, wg_ref, wu_ref, wd_ref, fg_ref, o_ref, xn_ref, acc_ref, *, final):
    f = pl.program_id(1)

    @pl.when(f == 0)
    def _():
        xn_ref[...] = _rms(x_ref[...], g_ref[...]).astype(BF16)
        acc_ref[...] = jnp.zeros_like(acc_ref)

    xn = xn_ref[...]
    h = jnp.dot(xn, wg_ref[...], preferred_element_type=F32)
    u = jnp.dot(xn, wu_ref[...], preferred_element_type=F32)
    a = (h * jax.nn.sigmoid(h) * u).astype(BF16)
    acc_ref[...] += jnp.dot(a, wd_ref[...], preferred_element_type=F32)

    @pl.when(f == pl.num_programs(1) - 1)
    def _():
        y = x_ref[...] + FFN_RES * acc_ref[...]
        if final:
            y = _rms(y, fg_ref[...])
        o_ref[...] = y


def _ffn(x, gain, wg, wu, wd, layer, final_gain=None, *, tm=512, tf=512):
    m, d = x.shape
    f = wg.shape[-1]
    final = final_gain is not None
    fg = final_gain if final else gain[layer]
    return pl.pallas_call(
        functools.partial(_ffn_kernel, final=final),
        grid=(m // tm, f // tf),
        in_specs=[
            pl.BlockSpec((tm, d), lambda i, j: (i, 0)),
            pl.BlockSpec((1, d), lambda i, j: (0, 0)),
            pl.BlockSpec((None, d, tf), lambda i, j: (layer, 0, j)),
            pl.BlockSpec((None, d, tf), lambda i, j: (layer, 0, j)),
            pl.BlockSpec((None, tf, d), lambda i, j: (layer, j, 0)),
            pl.BlockSpec((1, d), lambda i, j: (0, 0)),
        ],
        out_specs=pl.BlockSpec((tm, d), lambda i, j: (i, 0)),
        out_shape=jax.ShapeDtypeStruct((m, d), F32),
        scratch_shapes=[pltpu.VMEM((tm, d), BF16), pltpu.VMEM((tm, d), F32)],
        compiler_params=_params("parallel", "arbitrary"),
        name="ffn",
    )(x, gain[layer].reshape(1, d), wg, wu, wd, fg.reshape(1, d))


def _ln_silu(c, g, b):
    mu = jnp.mean(c, axis=-1, keepdims=True)
    xc = c - mu
    y = xc * lax.rsqrt(jnp.mean(xc * xc, axis=-1, keepdims=True) + EPS) * g + b
    return y * jax.nn.sigmoid(y)


def _proj_kernel(*refs, norm, bias, glu, res, n_out):
    refs = list(refs)
    x_ref = refs.pop(0)
    g_ref = refs.pop(0) if norm else None
    nb_ref = refs.pop(0) if norm == "ln_silu" else None
    w_refs = [refs.pop(0) for _ in range(2 if glu else 1)]
    b_refs = [refs.pop(0) for _ in range(2 if glu else 1)] if bias else None
    r_ref = refs.pop(0) if res else None
    o_refs = [refs.pop(0) for _ in range(n_out)]
    xn_ref = refs.pop(0) if norm else None

    if norm:
        @pl.when(pl.program_id(1) == 0)
        def _():
            if norm == "rms":
                xn = _rms(x_ref[...], g_ref[...])
            else:
                xn = _ln_silu(x_ref[...], g_ref[...], nb_ref[...])
            xn_ref[...] = xn.astype(BF16)
        xn = xn_ref[...]
    else:
        xn = x_ref[...]

    ys = [jnp.dot(xn, w[...], preferred_element_type=F32) for w in w_refs]
    if bias:
        ys = [y + b[...] for y, b in zip(ys, b_refs)]
    y = ys[0] * jax.nn.sigmoid(ys[1]) if glu else ys[0]
    if res:
        y = r_ref[...] + y
    for o in o_refs:
        o[...] = y.astype(o.dtype)


def _proj(x, w, *, gain=None, ln_bias=None, bias=None, glu=False, residual=None, out_dtypes=(F32,),
          tm=512, tn=1024):
    m, d = x.shape
    n = w.shape[1] // 2 if glu else w.shape[1]
    tn = min(tn, n)
    nb = n // tn
    norm = None if gain is None else ("rms" if ln_bias is None else "ln_silu")
    args = [x]
    specs = [pl.BlockSpec((tm, d), lambda i, j: (i, 0))]
    if norm:
        args.append(gain.reshape(1, d))
        specs.append(pl.BlockSpec((1, d), lambda i, j: (0, 0)))
    if norm == "ln_silu":
        args.append(ln_bias.reshape(1, d))
        specs.append(pl.BlockSpec((1, d), lambda i, j: (0, 0)))
    args.append(w)
    specs.append(pl.BlockSpec((d, tn), lambda i, j: (0, j)))
    if glu:
        args.append(w)
        specs.append(pl.BlockSpec((d, tn), lambda i, j: (0, j + nb)))
    if bias is not None:
        b2 = bias.reshape(1, -1)
        args.append(b2)
        specs.append(pl.BlockSpec((1, tn), lambda i, j: (0, j)))
        if glu:
            args.append(b2)
            specs.append(pl.BlockSpec((1, tn), lambda i, j: (0, j + nb)))
    if residual is not None:
        args.append(residual)
        specs.append(pl.BlockSpec((tm, tn), lambda i, j: (i, j)))
    out = pl.pallas_call(
        functools.partial(_proj_kernel, norm=norm, bias=bias is not None, glu=glu,
                          res=residual is not None, n_out=len(out_dtypes)),
        grid=(m // tm, nb),
        in_specs=specs,
        out_specs=[pl.BlockSpec((tm, tn), lambda i, j: (i, j)) for _ in out_dtypes],
        out_shape=[jax.ShapeDtypeStruct((m, n), dt) for dt in out_dtypes],
        scratch_shapes=[pltpu.VMEM((tm, d), BF16)] if norm else [],
        compiler_params=_params("parallel", "arbitrary"),
        name="proj",
    )(*args)
    return out if len(out_dtypes) > 1 else out[0]


def _conv_kernel(u_ref, st_ref, dw_ref, db_ref, o_ref, ext_ref, *, tt, rows):
    t = pl.program_id(2)
    lo = HALO_PAD - HALO

    @pl.when(t == 0)
    def _():
        ext_ref[lo:HALO_PAD, :] = st_ref[0]

    @pl.when(t > 0)
    def _():
        ext_ref[lo:HALO_PAD, :] = ext_ref[tt + lo:tt + HALO_PAD, :]

    ext_ref[HALO_PAD:, :] = u_ref[0]

    for r0 in range(0, tt, rows):
        acc = ext_ref[r0 + lo:r0 + lo + rows, :] * dw_ref[0:1, :] + db_ref[...]
        for w in range(1, CONV_WIDTH):
            acc = acc + ext_ref[r0 + lo + w:r0 + lo + w + rows, :] * dw_ref[w:w + 1, :]
        o_ref[0, r0:r0 + rows, :] = acc


def _conv(u, state, dw_w, dw_b, *, tt, cols=256):
    b, t, d = u.shape
    return pl.pallas_call(
        functools.partial(_conv_kernel, tt=tt, rows=min(tt, 64)),
        grid=(b, d // cols, t // tt),
        in_specs=[
            pl.BlockSpec((1, tt, cols), lambda i, c, j: (i, j, c)),
            pl.BlockSpec((1, HALO, cols), lambda i, c, j: (i, 0, c)),
            pl.BlockSpec((CONV_WIDTH, cols), lambda i, c, j: (0, c)),
            pl.BlockSpec((1, cols), lambda i, c, j: (0, c)),
        ],
        out_specs=pl.BlockSpec((1, tt, cols), lambda i, c, j: (i, j, c)),
        out_shape=jax.ShapeDtypeStruct((b, t, d), F32),
        scratch_shapes=[pltpu.VMEM((HALO_PAD + tt, cols), F32)],
        compiler_params=_params("parallel", "parallel", "arbitrary"),
        name="conv",
    )(u, state, dw_w, dw_b.reshape(1, d))


def _softplus(z):
    return jnp.maximum(z, 0.0) + jnp.log(1.0 + jnp.exp(-jnp.abs(z)))


def _suffix_sum(l, tri):
    hi = l.astype(BF16)
    lo = (l - hi.astype(F32)).astype(BF16)
    return (jnp.dot(hi, tri, preferred_element_type=F32) + jnp.dot(lo, tri, preferred_element_type=F32))


def _attn_kernel(q_ref, kd_ref, vd_ref, kp_ref, vp_ref, tri_ref, o_ref, acc_ref, carry_ref,
                 *, tq, tk, causal_past):
    q = q_ref[0]
    nt = (((1,), (1,)), ((), ()))

    z = lax.dot_general(q, kd_ref[0], nt, preferred_element_type=F32) * SCALE
    mask = lax.broadcasted_iota(jnp.int32, (tq, tq), 1) < lax.broadcasted_iota(jnp.int32, (tq, tq), 0)
    l = jnp.where(mask, -_softplus(z), 0.0)
    s = _suffix_sum(l, tri_ref[:tq, :tq])
    a = jnp.where(mask, jnp.exp(z + l + s), 0.0)
    acc_ref[...] = jnp.dot(a.astype(BF16), vd_ref[0], preferred_element_type=F32)
    carry_ref[...] = jnp.sum(l, axis=1, keepdims=True)

    n_past = pl.program_id(2) * (tq // tk) if causal_past else kp_ref.shape[1] // tk
    tri = tri_ref[...]

    def body(i, _):
        off = pl.multiple_of((n_past - 1 - i) * tk, tk)
        k = kp_ref[0, pl.ds(off, tk), :]
        v = vp_ref[0, pl.ds(off, tk), :]
        z = lax.dot_general(q, k, nt, preferred_element_type=F32) * SCALE
        l = -_softplus(z)
        a = jnp.exp(z + l + _suffix_sum(l, tri) + carry_ref[...])
        acc_ref[...] += jnp.dot(a.astype(BF16), v, preferred_element_type=F32)
        carry_ref[...] += jnp.sum(l, axis=1, keepdims=True)
        return 0

    lax.fori_loop(0, n_past, body, 0)
    o_ref[0] = acc_ref[...].astype(o_ref.dtype)


def _attention(q, k_new, v_new, k_past, v_past, tri, *, tq, tk, causal_past):
    b, t, _ = q.shape
    p = k_past.shape[1]
    qspec = pl.BlockSpec((1, tq, HEAD_DIM), lambda bi, h, i: (bi, i, h))
    pspec = pl.BlockSpec((1, p, HEAD_DIM), lambda bi, h, i: (bi, 0, h))
    return pl.pallas_call(
        functools.partial(_attn_kernel, tq=tq, tk=tk, causal_past=causal_past),
        grid=(b, N_HEADS, t // tq),
        in_specs=[qspec, qspec, qspec, pspec, pspec, pl.BlockSpec((tk, tk), lambda bi, h, i: (0, 0))],
        out_specs=qspec,
        out_shape=jax.ShapeDtypeStruct(q.shape, BF16),
        scratch_shapes=[pltpu.VMEM((tq, HEAD_DIM), F32), pltpu.VMEM((tq, 1), F32)],
        compiler_params=_params("parallel", "parallel", "arbitrary"),
        name="attention",
    )(q, k_new, v_new, k_past, v_past, tri)


def _trunk(x, conv_state, cache_k, cache_v, w, tri, *, conv_tt, tq, tk):
    b, t, d = x.shape
    m = b * t
    hk = N_HEADS * HEAD_DIM
    x = x.reshape(m, d)

    x = _ffn(x, w["ffn1_norm"], w["ffn1_w_gate"], w["ffn1_w_up"], w["ffn1_w_down"], 0)
    u = _proj(x, w["conv_pw1_w"][0], gain=w["mix_norm"][0], bias=w["conv_pw1_b"][0], glu=True)
    u3 = u.reshape(b, t, d)
    new_state = u3[:, t - HALO:]
    c = _conv(u3, conv_state, w["conv_dw_w"][0], w["conv_dw_b"][0], tt=conv_tt)
    x = _proj(c.reshape(m, d), w["conv_pw2_w"][0], gain=w["conv_ln_g"][0], ln_bias=w["conv_ln_b"][0],
              bias=w["conv_pw2_b"][0], residual=x)
    x = _ffn(x, w["ffn2_norm"], w["ffn2_w_gate"], w["ffn2_w_up"], w["ffn2_w_down"], 0)

    kv, kv16 = _proj(x, w["w_kv"], gain=w["kv_norm"], out_dtypes=(F32, BF16))
    k_new = kv[:, :hk].reshape(b, t, N_HEADS, HEAD_DIM)
    v_new = kv[:, hk:].reshape(b, t, N_HEADS, HEAD_DIM)
    k16 = kv16[:, :hk].reshape(b, t, hk)
    v16 = kv16[:, hk:].reshape(b, t, hk)

    x = _ffn(x, w["ffn1_norm"], w["ffn1_w_gate"], w["ffn1_w_up"], w["ffn1_w_down"], 1)
    q = _proj(x, w["attn_wq"][0], gain=w["mix_norm"][1], out_dtypes=(BF16,)).reshape(b, t, hk)
    if cache_k is None:
        o = _attention(q, k16, v16, k16, v16, tri, tq=tq, tk=tk, causal_past=True)
    else:
        o = _attention(q, k16, v16, cache_k, cache_v, tri, tq=tq, tk=tk, causal_past=False)
    x = _proj(o.reshape(m, hk), w["attn_wo"][0], residual=x)
    y = _ffn(x, w["ffn2_norm"], w["ffn2_w_gate"], w["ffn2_w_up"], w["ffn2_w_down"], 1, w["final_norm"])
    return y.reshape(b, t, d), new_state[None], k_new, v_new


def kernel(x_prompt, x_sample, state_conv, cache_k, cache_v, ffn1_norm, ffn1_w_gate, ffn1_w_up, ffn1_w_down, mix_norm, ffn2_norm, ffn2_w_gate, ffn2_w_up, ffn2_w_down, conv_pw1_w, conv_pw1_b, conv_dw_w, conv_dw_b, conv_ln_g, conv_ln_b, conv_pw2_w, conv_pw2_b, kv_norm, w_kv, attn_wq, attn_wo, final_norm):
    w = dict(
        ffn1_norm=ffn1_norm, mix_norm=mix_norm, ffn2_norm=ffn2_norm, kv_norm=kv_norm, final_norm=final_norm,
        conv_pw1_b=conv_pw1_b, conv_dw_w=conv_dw_w, conv_dw_b=conv_dw_b, conv_ln_g=conv_ln_g,
        conv_ln_b=conv_ln_b, conv_pw2_b=conv_pw2_b,
        ffn1_w_gate=ffn1_w_gate.astype(BF16), ffn1_w_up=ffn1_w_up.astype(BF16),
        ffn1_w_down=ffn1_w_down.astype(BF16), ffn2_w_gate=ffn2_w_gate.astype(BF16),
        ffn2_w_up=ffn2_w_up.astype(BF16), ffn2_w_down=ffn2_w_down.astype(BF16),
        conv_pw1_w=conv_pw1_w.astype(BF16), conv_pw2_w=conv_pw2_w.astype(BF16),
        w_kv=w_kv.astype(BF16), attn_wq=attn_wq.astype(BF16), attn_wo=attn_wo.astype(BF16),
    )
    tk = 256
    tri = (lax.broadcasted_iota(jnp.int32, (tk, tk), 0) > lax.broadcasted_iota(jnp.int32, (tk, tk), 1)).astype(BF16)

    bp, tp, _ = x_prompt.shape
    bs, ts, _ = x_sample.shape
    past = cache_k.shape[1]
    hk = N_HEADS * HEAD_DIM
    zero_state = jnp.zeros((bp, HALO, D_MODEL), F32)
    y_p, st_p, k_p, v_p = _trunk(x_prompt, zero_state, None, None, w, tri, conv_tt=512, tq=256, tk=tk)
    y_s, st_s, k_s, v_s = _trunk(x_sample, state_conv[0], cache_k.reshape(bs, past, hk).astype(BF16),
                                 cache_v.reshape(bs, past, hk).astype(BF16), w, tri, conv_tt=ts, tq=ts, tk=tk)
    return (y_p, y_s, st_p, k_p, v_p, st_s, k_s, v_s)
```

```python
import functools

import jax
import jax.numpy as jnp
from jax import lax
from jax.experimental import pallas as pl
from jax.experimental.pallas import tpu as pltpu

D_MODEL = 2048
N_HEADS = 16
HEAD_DIM = 128
CONV_WIDTH = 31
EPS = 1e-6
FFN_RES = 0.5
LOG2E = 1.4426950408889634
Q_SCALE = HEAD_DIM ** -0.5 * LOG2E
EXIT_LOG2 = 150.0
SIGN_BIT = -2 ** 31

SUBLANES = 8
HALO = CONV_WIDTH - 1
HALO_PAD = 32

VMEM_LIMIT = 56 * 1024 * 1024

F32 = jnp.float32
BF16 = jnp.bfloat16


def _params(*sem):
    return pltpu.CompilerParams(dimension_semantics=sem, vmem_limit_bytes=VMEM_LIMIT)


def _rms(x, g):
    return x * lax.rsqrt(jnp.mean(x * x, axis=-1, keepdims=True) + EPS) * g


def _ln_silu(c, g, b):
    mu = jnp.mean(c, axis=-1, keepdims=True)
    xc = c - mu
    y = xc * lax.rsqrt(jnp.mean(xc * xc, axis=-1, keepdims=True) + EPS) * g + b
    return y * jax.nn.sigmoid(y)


def _normalize_rows(x_ref, xn_ref, fn, chunk=256):
    chunk = min(chunk, x_ref.shape[0])

    def body(r, carry):
        rows = pl.ds(pl.multiple_of(r * chunk, chunk), chunk)
        xn_ref[rows, :] = fn(x_ref[rows, :]).astype(BF16)
        return carry

    lax.fori_loop(0, x_ref.shape[0] // chunk, body, 0)


def _ffn_kernel(x_ref, g_ref, wg_ref, wu_ref, wd_ref, fg_ref, o_ref, xn_ref, acc_ref, *, final):
    f = pl.program_id(1)

    @pl.when(f == 0)
    def _():
        _normalize_rows(x_ref, xn_ref, lambda x: _rms(x, g_ref[...]))
        acc_ref[...] = jnp.zeros_like(acc_ref)

    xn = xn_ref[...]
    h = jnp.dot(xn, wg_ref[...], preferred_element_type=F32)
    u = jnp.dot(xn, wu_ref[...], preferred_element_type=F32)
    a = (h * jax.nn.sigmoid(h) * u).astype(BF16)
    acc_ref[...] += jnp.dot(a, wd_ref[...], preferred_element_type=F32)

    @pl.when(f == pl.num_programs(1) - 1)
    def _():
        y = x_ref[...] + FFN_RES * acc_ref[...]
        if final:
            y = _rms(y, fg_ref[...])
        o_ref[...] = y


def _ffn(x, gain, wg, wu, wd, layer, final_gain=None, *, tm=512, tf=512):
    m, d = x.shape
    f = wg.shape[-1]
    final = final_gain is not None
    fg = final_gain if final else gain[layer]
    return pl.pallas_call(
        functools.partial(_ffn_kernel, final=final),
        grid=(m // tm, f // tf),
        in_specs=[
            pl.BlockSpec((tm, d), lambda i, j: (i, 0)),
            pl.BlockSpec((1, d), lambda i, j: (0, 0)),
            pl.BlockSpec((None, d, tf), lambda i, j: (layer, 0, j)),
            pl.BlockSpec((None, d, tf), lambda i, j: (layer, 0, j)),
            pl.BlockSpec((None, tf, d), lambda i, j: (layer, j, 0)),
            pl.BlockSpec((1, d), lambda i, j: (0, 0)),
        ],
        out_specs=pl.BlockSpec((tm, d), lambda i, j: (i, 0)),
        out_shape=jax.ShapeDtypeStruct((m, d), F32),
        scratch_shapes=[pltpu.VMEM((tm, d), BF16), pltpu.VMEM((tm, d), F32)],
        compiler_params=_params("parallel", "arbitrary"),
        name="ffn",
    )(x, gain[layer].reshape(1, d), wg, wu, wd, fg.reshape(1, d))


def _proj_kernel(*refs, norm, bias, pair, res, out_scale, n_dt):
    refs = list(refs)
    x_ref = refs.pop(0)
    g_ref = refs.pop(0) if norm else None
    nb_ref = refs.pop(0) if norm == "ln_silu" else None
    n_w = 2 if pair else 1
    w_refs = [refs.pop(0) for _ in range(n_w)]
    b_refs = [refs.pop(0) for _ in range(n_w)] if bias else None
    r_ref = refs.pop(0) if res else None
    n_y = 2 if pair == "split" else 1
    o_refs = [[refs.pop(0) for _ in range(n_dt)] for _ in range(n_y)]
    xn_ref = refs.pop(0) if norm else None

    if norm:
        @pl.when(pl.program_id(1) == 0)
        def _():
            if norm == "rms":
                _normalize_rows(x_ref, xn_ref, lambda x: _rms(x, g_ref[...]))
            else:
                _normalize_rows(x_ref, xn_ref, lambda x: _ln_silu(x, g_ref[...], nb_ref[...]))
        xn = xn_ref[...]
    else:
        xn = x_ref[...]

    ys = [jnp.dot(xn, w[...], preferred_element_type=F32) for w in w_refs]
    if bias:
        ys = [y + b[...] for y, b in zip(ys, b_refs)]
    if pair == "glu":
        ys = [ys[0] * jax.nn.sigmoid(ys[1])]
    if res:
        ys = [r_ref[...] + ys[0]]
    if out_scale is not None:
        ys = [y * out_scale for y in ys]
    for y, outs in zip(ys, o_refs):
        for o in outs:
            o[...] = y.astype(o.dtype)


def _proj(x, w, *, gain=None, ln_bias=None, bias=None, pair=None, residual=None, out_scale=None,
          out_dtypes=(F32,), tm=1024, tn=1024):
    m, d = x.shape
    n = w.shape[1] // 2 if pair else w.shape[1]
    tm, tn = min(tm, m), min(tn, n)
    nb = n // tn
    norm = None if gain is None else ("rms" if ln_bias is None else "ln_silu")
    args = [x]
    specs = [pl.BlockSpec((tm, d), lambda i, j: (i, 0))]
    if norm:
        args.append(gain.reshape(1, d))
        specs.append(pl.BlockSpec((1, d), lambda i, j: (0, 0)))
    if norm == "ln_silu":
        args.append(ln_bias.reshape(1, d))
        specs.append(pl.BlockSpec((1, d), lambda i, j: (0, 0)))
    args.append(w)
    specs.append(pl.BlockSpec((d, tn), lambda i, j: (0, j)))
    if pair:
        args.append(w)
        specs.append(pl.BlockSpec((d, tn), lambda i, j: (0, j + nb)))
    if bias is not None:
        b2 = bias.reshape(1, -1)
        args.append(b2)
        specs.append(pl.BlockSpec((1, tn), lambda i, j: (0, j)))
        if pair:
            args.append(b2)
            specs.append(pl.BlockSpec((1, tn), lambda i, j: (0, j + nb)))
    if residual is not None:
        args.append(residual)
        specs.append(pl.BlockSpec((tm, tn), lambda i, j: (i, j)))
    n_out = (2 if pair == "split" else 1) * len(out_dtypes)
    dts = list(out_dtypes) * (n_out // len(out_dtypes))
    out = pl.pallas_call(
        functools.partial(_proj_kernel, norm=norm, bias=bias is not None, pair=pair,
                          res=residual is not None, out_scale=out_scale, n_dt=len(out_dtypes)),
        grid=(m // tm, nb),
        in_specs=specs,
        out_specs=[pl.BlockSpec((tm, tn), lambda i, j: (i, j)) for _ in dts],
        out_shape=[jax.ShapeDtypeStruct((m, n), dt) for dt in dts],
        scratch_shapes=[pltpu.VMEM((tm, d), BF16)] if norm else [],
        compiler_params=_params("parallel", "arbitrary"),
        name="proj",
    )(*args)
    return out if n_out > 1 else out[0]


def _conv_kernel(u_ref, st_ref, dw_ref, db_ref, o_ref, ext_ref, *, tt, rows, lanes):
    t = pl.program_id(2)
    lo = HALO_PAD - HALO

    @pl.when(t == 0)
    def _():
        ext_ref[0:lo, :] = jnp.zeros((lo, ext_ref.shape[1]), F32)
        ext_ref[lo:HALO_PAD, :] = st_ref[0]

    @pl.when(t > 0)
    def _():
        ext_ref[0:HALO_PAD, :] = ext_ref[tt:tt + HALO_PAD, :]

    ext_ref[HALO_PAD:, :] = u_ref[0]

    taps = [[] for _ in range(SUBLANES)]
    for w in range(CONV_WIDTH):
        taps[(lo + w) % SUBLANES].append(((lo + w) // SUBLANES, w))
    for c0 in range(0, ext_ref.shape[1], lanes):
        cs = slice(c0, c0 + lanes)
        for r0 in range(0, tt, rows):
            acc = None
            for b in range(SUBLANES):
                n = rows + SUBLANES if b else rows
                part = None
                for a, w in taps[b]:
                    start = r0 + a * SUBLANES
                    term = ext_ref[start:start + n, cs] * dw_ref[w:w + 1, cs]
                    part = term if part is None else part + term
                part = part[b:b + rows]
                acc = part + db_ref[:, cs] if acc is None else acc + part
            o_ref[0, r0:r0 + rows, cs] = acc


def _conv(u, state, dw_w, dw_b, *, tt, cols=256):
    b, t, d = u.shape
    return pl.pallas_call(
        functools.partial(_conv_kernel, tt=tt, rows=min(tt, 128), lanes=128),
        grid=(b, d // cols, t // tt),
        in_specs=[
            pl.BlockSpec((1, tt, cols), lambda i, c, j: (i, j, c)),
            pl.BlockSpec((1, HALO, cols), lambda i, c, j: (i, 0, c)),
            pl.BlockSpec((CONV_WIDTH, cols), lambda i, c, j: (0, c)),
            pl.BlockSpec((1, cols), lambda i, c, j: (0, c)),
        ],
        out_specs=pl.BlockSpec((1, tt, cols), lambda i, c, j: (i, j, c)),
        out_shape=jax.ShapeDtypeStruct((b, t, d), F32),
        scratch_shapes=[pltpu.VMEM((HALO_PAD + tt, cols), F32)],
        compiler_params=_params("parallel", "parallel", "arbitrary"),
        name="conv",
    )(u, state, dw_w, dw_b.reshape(1, d))


def _suffix_matrix(n):
    tri = lax.broadcasted_iota(jnp.int32, (n, n), 0) > lax.broadcasted_iota(jnp.int32, (n, n), 1)
    return jnp.concatenate([tri, tri], axis=0).astype(BF16)


def _attn_kernel(q_ref, kd_ref, vd_ref, kp_ref, vp_ref, trid_ref, trip_ref, o_ref, acc_ref, mass_ref,
                 *, g, tq, tk, causal_past):
    nt = (((1,), (1,)), ((), ()))
    hd = HEAD_DIM
    qs = [q_ref[0, :, h * hd:(h + 1) * hd] for h in range(g)]

    def block(ks, vs, tri2, mask, first):
        zs, sps = [], []
        for h in range(g):
            z = lax.dot_general(qs[h], ks[h], nt, preferred_element_type=F32)
            neg_abs = lax.bitcast_convert_type(lax.bitcast_convert_type(z, jnp.int32) | SIGN_BIT, F32)
            sp = jnp.maximum(z, 0.0) + jnp.log(1.0 + jnp.exp2(neg_abs)) * LOG2E
            if mask is not None:
                sp = jnp.where(mask, sp, 0.0)
            zs.append(z)
            sps.append(sp)
        z = jnp.concatenate(zs, axis=0)
        sp = jnp.concatenate(sps, axis=0)
        rowsum = jnp.sum(sp, axis=1, keepdims=True)
        hi = sp.astype(BF16)
        lo = (sp - hi.astype(F32)).astype(BF16)
        n = sp.shape[1]
        if n % 128 == 0:
            suffix = jnp.dot(jnp.concatenate([hi, lo], axis=1), tri2, preferred_element_type=F32)
        else:
            suffix = (jnp.dot(hi, tri2[:n], preferred_element_type=F32)
                      + jnp.dot(lo, tri2[n:], preferred_element_type=F32))
        x = z - sp - suffix
        if not first:
            x = x - mass_ref[...]
        a = jnp.exp2(x)
        for h in range(g):
            ah = a[h * tq:(h + 1) * tq]
            if mask is not None:
                ah = jnp.where(mask, ah, 0.0)
            pv = jnp.dot(ah.astype(BF16), vs[h], preferred_element_type=F32)
            if first:
                acc_ref[:, h * hd:(h + 1) * hd] = pv
            else:
                acc_ref[:, h * hd:(h + 1) * hd] += pv
        if first:
            mass_ref[...] = rowsum
        else:
            mass_ref[...] += rowsum

    mask = lax.broadcasted_iota(jnp.int32, (tq, tq), 1) < lax.broadcasted_iota(jnp.int32, (tq, tq), 0)
    block([kd_ref[0, :, h * hd:(h + 1) * hd] for h in range(g)],
          [vd_ref[0, :, h * hd:(h + 1) * hd] for h in range(g)], trid_ref[...], mask, True)

    n_past = pl.program_id(2) * (tq // tk) if causal_past else kp_ref.shape[1] // tk
    tri2 = trip_ref[...]

    def cond(c):
        i, min_mass = c
        return jnp.logical_and(i < n_past, min_mass < EXIT_LOG2)

    def body(c):
        i, _ = c
        off = pl.multiple_of((n_past - 1 - i) * tk, tk)
        block([kp_ref[0, pl.ds(off, tk), h * hd:(h + 1) * hd].astype(BF16) for h in range(g)],
              [vp_ref[0, pl.ds(off, tk), h * hd:(h + 1) * hd].astype(BF16) for h in range(g)], tri2, None, False)
        return i + 1, jnp.min(mass_ref[...])

    lax.while_loop(cond, body, (jnp.int32(0), jnp.min(mass_ref[...])))
    o_ref[0] = acc_ref[...].astype(o_ref.dtype)


def _attention(q, k_new, v_new, k_past, v_past, *, g, tq, tk, causal_past):
    b, t, _ = q.shape
    p = k_past.shape[1]
    gw = g * HEAD_DIM
    qspec = pl.BlockSpec((1, tq, gw), lambda bi, h, i: (bi, i, h))
    pspec = pl.BlockSpec((1, p, gw), lambda bi, h, i: (bi, 0, h))
    return pl.pallas_call(
        functools.partial(_attn_kernel, g=g, tq=tq, tk=tk, causal_past=causal_past),
        grid=(b, N_HEADS // g, t // tq),
        in_specs=[qspec, qspec, qspec, pspec, pspec,
                  pl.BlockSpec((2 * tq, tq), lambda bi, h, i: (0, 0)),
                  pl.BlockSpec((2 * tk, tk), lambda bi, h, i: (0, 0))],
        out_specs=qspec,
        out_shape=jax.ShapeDtypeStruct(q.shape, BF16),
        scratch_shapes=[pltpu.VMEM((tq, gw), F32), pltpu.VMEM((g * tq, 1), F32)],
        compiler_params=_params("parallel", "parallel", "arbitrary"),
        name="attention",
    )(q, k_new, v_new, k_past, v_past, _suffix_matrix(tq), _suffix_matrix(tk))


def _trunk(x, conv_state, cache_k, cache_v, w, *, conv_tt, attn_g, tq, tk):
    b, t, d = x.shape
    m = b * t
    hk = N_HEADS * HEAD_DIM
    x = x.reshape(m, d)

    x = _ffn(x, w["ffn1_norm"], w["ffn1_w_gate"], w["ffn1_w_up"], w["ffn1_w_down"], 0)
    u = _proj(x, w["conv_pw1_w"][0], gain=w["mix_norm"][0], bias=w["conv_pw1_b"][0], pair="glu", tn=512)
    u = u.reshape(b, t, d)
    new_state = u[:, t - HALO:]
    c = _conv(u, conv_state, w["conv_dw_w"][0], w["conv_dw_b"][0], tt=conv_tt)
    x = _proj(c.reshape(m, d), w["conv_pw2_w"][0], gain=w["conv_ln_g"][0], ln_bias=w["conv_ln_b"][0],
              bias=w["conv_pw2_b"][0], residual=x)
    x = _ffn(x, w["ffn2_norm"], w["ffn2_w_gate"], w["ffn2_w_up"], w["ffn2_w_down"], 0)

    k_new, k16, v_new, v16 = _proj(x, w["w_kv"], gain=w["kv_norm"], pair="split", out_dtypes=(F32, BF16),
                                   tn=512)

    x = _ffn(x, w["ffn1_norm"], w["ffn1_w_gate"], w["ffn1_w_up"], w["ffn1_w_down"], 1)
    q = _proj(x, w["attn_wq"][0], gain=w["mix_norm"][1], out_scale=Q_SCALE, out_dtypes=(BF16,))
    q, k16, v16 = (a.reshape(b, t, hk) for a in (q, k16, v16))
    if cache_k is None:
        o = _attention(q, k16, v16, k16, v16, g=attn_g, tq=tq, tk=tk, causal_past=True)
    else:
        o = _attention(q, k16, v16, cache_k, cache_v, g=attn_g, tq=tq, tk=tk, causal_past=False)
    x = _proj(o.reshape(m, hk), w["attn_wo"][0], residual=x)
    y = _ffn(x, w["ffn2_norm"], w["ffn2_w_gate"], w["ffn2_w_up"], w["ffn2_w_down"], 1, w["final_norm"])
    return (y.reshape(b, t, d), new_state[None], k_new.reshape(b, t, N_HEADS, HEAD_DIM),
            v_new.reshape(b, t, N_HEADS, HEAD_DIM))


def kernel(x_prompt, x_sample, state_conv, cache_k, cache_v, ffn1_norm, ffn1_w_gate, ffn1_w_up, ffn1_w_down, mix_norm, ffn2_norm, ffn2_w_gate, ffn2_w_up, ffn2_w_down, conv_pw1_w, conv_pw1_b, conv_dw_w, conv_dw_b, conv_ln_g, conv_ln_b, conv_pw2_w, conv_pw2_b, kv_norm, w_kv, attn_wq, attn_wo, final_norm):
    w = dict(
        ffn1_norm=ffn1_norm, mix_norm=mix_norm, ffn2_norm=ffn2_norm, kv_norm=kv_norm, final_norm=final_norm,
        conv_pw1_b=conv_pw1_b, conv_dw_w=conv_dw_w, conv_dw_b=conv_dw_b, conv_ln_g=conv_ln_g,
        conv_ln_b=conv_ln_b, conv_pw2_b=conv_pw2_b,
        ffn1_w_gate=ffn1_w_gate.astype(BF16), ffn1_w_up=ffn1_w_up.astype(BF16),
        ffn1_w_down=ffn1_w_down.astype(BF16), ffn2_w_gate=ffn2_w_gate.astype(BF16),
        ffn2_w_up=ffn2_w_up.astype(BF16), ffn2_w_down=ffn2_w_down.astype(BF16),
        conv_pw1_w=conv_pw1_w.astype(BF16), conv_pw2_w=conv_pw2_w.astype(BF16),
        w_kv=w_kv.astype(BF16), attn_wq=attn_wq.astype(BF16), attn_wo=attn_wo.astype(BF16),
    )
    bp, tp, _ = x_prompt.shape
    bs, ts, _ = x_sample.shape
    past = cache_k.shape[1]
    hk = N_HEADS * HEAD_DIM
    zero_state = jnp.zeros((bp, HALO, D_MODEL), F32)
    y_p, st_p, k_p, v_p = _trunk(x_prompt, zero_state, None, None, w, conv_tt=512, attn_g=4, tq=256, tk=256)
    y_s, st_s, k_s, v_s = _trunk(x_sample, state_conv[0], cache_k.reshape(bs, past, hk),
                                 cache_v.reshape(bs, past, hk), w, conv_tt=ts, attn_g=N_HEADS, tq=ts, tk=256)
    return (y_p, y_s, st_p, k_p, v_p, st_s, k_s, v_s)
```

```python
import functools

import jax
import jax.numpy as jnp
from jax import lax
from jax.experimental import pallas as pl
from jax.experimental.pallas import tpu as pltpu

D_MODEL = 2048
N_HEADS = 16
HEAD_DIM = 128
CONV_WIDTH = 31
EPS = 1e-6
FFN_RES = 0.5
LOG2E = 1.4426950408889634
Q_SCALE = HEAD_DIM ** -0.5 * LOG2E
EXIT_LOG2 = 150.0
SIGN_BIT = -2 ** 31

SUBLANES = 8
HALO = CONV_WIDTH - 1
HALO_PAD = 32

VMEM_LIMIT = 56 * 1024 * 1024

F32 = jnp.float32
BF16 = jnp.bfloat16


def _params(*sem):
    return pltpu.CompilerParams(dimension_semantics=sem, vmem_limit_bytes=VMEM_LIMIT)


def _rms(x, g):
    return x * lax.rsqrt(jnp.mean(x * x, axis=-1, keepdims=True) + EPS) * g


def _ln_silu(c, g, b):
    mu = jnp.mean(c, axis=-1, keepdims=True)
    xc = c - mu
    y = xc * lax.rsqrt(jnp.mean(xc * xc, axis=-1, keepdims=True) + EPS) * g + b
    return y * jax.nn.sigmoid(y)


def _normalize_rows(x_ref, xn_ref, fn, chunk=256):
    chunk = min(chunk, x_ref.shape[0])

    def body(r, carry):
        rows = pl.ds(pl.multiple_of(r * chunk, chunk), chunk)
        xn_ref[rows, :] = fn(x_ref[rows, :]).astype(BF16)
        return carry

    lax.fori_loop(0, x_ref.shape[0] // chunk, body, 0)


def _ffn_kernel(x_ref, g_ref, wg_ref, wu_ref, wd_ref, fg_ref, o_ref, *rest, final, emit):
    w16_refs, xn_ref = rest[:-1], rest[-1]
    f = pl.program_id(1)

    @pl.when(f == 0)
    def _():
        _normalize_rows(x_ref, xn_ref, lambda x: _rms(x, g_ref[...]))
        o_ref[...] = jnp.zeros_like(o_ref)

    ws = [w[...] for w in (wg_ref, wu_ref, wd_ref)]
    if emit:
        ws = [w.astype(BF16) for w in ws]
        for w16_ref, w in zip(w16_refs, ws):
            w16_ref[...] = w
    wg, wu, wd = ws
    xn = xn_ref[...]
    h = jnp.dot(xn, wg, preferred_element_type=F32)
    u = jnp.dot(xn, wu, preferred_element_type=F32)
    a = (h * jax.nn.sigmoid(h) * u).astype(BF16)
    o_ref[...] += jnp.dot(a, wd, preferred_element_type=F32)

    @pl.when(f == pl.num_programs(1) - 1)
    def _():
        y = x_ref[...] + FFN_RES * o_ref[...]
        if final:
            y = _rms(y, fg_ref[...])
        o_ref[...] = y


def _ffn(x, gain, wg, wu, wd, layer, final_gain=None, *, tm=512, tf=512, emit=False):
    m, d = x.shape
    f = wg.shape[-1]
    final = final_gain is not None
    fg = final_gain if final else gain[layer]
    if emit:
        assert m == tm, "every weight tile must be visited exactly once"
        wspec = lambda r, c, imap: pl.BlockSpec((None, r, c), lambda i, j: (layer,) + imap(j))
    else:
        wspec = lambda r, c, imap: pl.BlockSpec((r, c), lambda i, j: imap(j))
    up_map, down_map = (lambda j: (0, j)), (lambda j: (j, 0))
    once = dict(pipeline_mode=pl.Buffered(1)) if m == tm else {}
    out_specs = [pl.BlockSpec((tm, d), lambda i, j: (i, 0), **once)]
    out_shape = [jax.ShapeDtypeStruct((m, d), F32)]
    if emit:
        out_specs += [pl.BlockSpec((d, tf), lambda i, j: (0, j)), pl.BlockSpec((d, tf), lambda i, j: (0, j)),
                      pl.BlockSpec((tf, d), lambda i, j: (j, 0))]
        out_shape += [jax.ShapeDtypeStruct(s, BF16) for s in ((d, f), (d, f), (f, d))]
    out = pl.pallas_call(
        functools.partial(_ffn_kernel, final=final, emit=emit),
        grid=(m // tm, f // tf),
        in_specs=[
            pl.BlockSpec((tm, d), lambda i, j: (i, 0), **once),
            pl.BlockSpec((1, d), lambda i, j: (0, 0)),
            wspec(d, tf, up_map), wspec(d, tf, up_map), wspec(tf, d, down_map),
            pl.BlockSpec((1, d), lambda i, j: (0, 0)),
        ],
        out_specs=out_specs,
        out_shape=out_shape,
        scratch_shapes=[pltpu.VMEM((tm, d), BF16)],
        compiler_params=_params("parallel", "arbitrary"),
        name="ffn",
    )(x, gain[layer].reshape(1, d), wg, wu, wd, fg.reshape(1, d))
    return out if emit else out[0]


def _proj_kernel(*refs, norm, bias, pair, res, out_scale, n_dt, emit):
    refs = list(refs)
    x_ref = refs.pop(0)
    g_ref = refs.pop(0) if norm else None
    nb_ref = refs.pop(0) if norm == "ln_silu" else None
    n_w = 2 if pair else 1
    w_refs = [refs.pop(0) for _ in range(n_w)]
    b_refs = [refs.pop(0) for _ in range(n_w)] if bias else None
    r_ref = refs.pop(0) if res else None
    n_y = 2 if pair == "split" else 1
    o_refs = [[refs.pop(0) for _ in range(n_dt)] for _ in range(n_y)]
    w16_refs = [refs.pop(0) for _ in range(n_w)] if emit else None
    xn_ref = refs.pop(0) if norm else None

    if norm:
        @pl.when(pl.program_id(1) == 0)
        def _():
            if norm == "rms":
                _normalize_rows(x_ref, xn_ref, lambda x: _rms(x, g_ref[...]))
            else:
                _normalize_rows(x_ref, xn_ref, lambda x: _ln_silu(x, g_ref[...], nb_ref[...]))
        xn = xn_ref[...]
    else:
        xn = x_ref[...]

    ws = [w[...] for w in w_refs]
    if emit:
        ws = [w.astype(BF16) for w in ws]
        for w16_ref, w in zip(w16_refs, ws):
            w16_ref[...] = w
    ys = [jnp.dot(xn, w, preferred_element_type=F32) for w in ws]
    if bias:
        ys = [y + b[...] for y, b in zip(ys, b_refs)]
    if pair == "glu":
        ys = [ys[0] * jax.nn.sigmoid(ys[1])]
    if res:
        ys = [r_ref[...] + ys[0]]
    if out_scale is not None:
        ys = [y * out_scale for y in ys]
    for y, outs in zip(ys, o_refs):
        for o in outs:
            o[...] = y.astype(o.dtype)


def _proj(x, w, *, gain=None, ln_bias=None, bias=None, pair=None, residual=None, out_scale=None,
          out_dtypes=(F32,), tm=1024, tn=1024, emit=False):
    m, d = x.shape
    halves = w if isinstance(w, tuple) else ((w, w) if pair else (w,))
    n = halves[0].shape[1] // (1 if isinstance(w, tuple) or not pair else 2)
    tm, tn = min(tm, m), min(tn, n)
    nb = n // tn
    assert not emit or m == tm, "every weight tile must be visited exactly once"
    norm = None if gain is None else ("rms" if ln_bias is None else "ln_silu")
    once = dict(pipeline_mode=pl.Buffered(1)) if m == tm else {}
    args = [x]
    specs = [pl.BlockSpec((tm, d), lambda i, j: (i, 0), **once)]
    if norm:
        args.append(gain.reshape(1, d))
        specs.append(pl.BlockSpec((1, d), lambda i, j: (0, 0)))
    if norm == "ln_silu":
        args.append(ln_bias.reshape(1, d))
        specs.append(pl.BlockSpec((1, d), lambda i, j: (0, 0)))
    for k, half in enumerate(halves):
        shift = 0 if isinstance(w, tuple) else k * nb
        args.append(half)
        specs.append(pl.BlockSpec((d, tn), lambda i, j, shift=shift: (0, j + shift)))
    if bias is not None:
        b2 = bias.reshape(1, -1)
        args.append(b2)
        specs.append(pl.BlockSpec((1, tn), lambda i, j: (0, j)))
        if pair:
            args.append(b2)
            specs.append(pl.BlockSpec((1, tn), lambda i, j: (0, j + nb)))
    if residual is not None:
        args.append(residual)
        specs.append(pl.BlockSpec((tm, tn), lambda i, j: (i, j)))
    n_out = (2 if pair == "split" else 1) * len(out_dtypes)
    dts = list(out_dtypes) * (n_out // len(out_dtypes))
    out_specs = [pl.BlockSpec((tm, tn), lambda i, j: (i, j)) for _ in dts]
    out_shape = [jax.ShapeDtypeStruct((m, n), dt) for dt in dts]
    if emit:
        out_specs += [pl.BlockSpec((d, tn), lambda i, j: (0, j)) for _ in halves]
        out_shape += [jax.ShapeDtypeStruct((d, n), BF16) for _ in halves]
    out = pl.pallas_call(
        functools.partial(_proj_kernel, norm=norm, bias=bias is not None, pair=pair, res=residual is not None,
                          out_scale=out_scale, n_dt=len(out_dtypes), emit=emit),
        grid=(m // tm, nb),
        in_specs=specs,
        out_specs=out_specs,
        out_shape=out_shape,
        scratch_shapes=[pltpu.VMEM((tm, d), BF16)] if norm else [],
        compiler_params=_params("parallel", "arbitrary"),
        name="proj",
    )(*args)
    return out if len(out) > 1 else out[0]


def _conv_kernel(u_ref, st_ref, dw_ref, db_ref, o_ref, ext_ref, *, tt, rows, lanes):
    t = pl.program_id(2)
    lo = HALO_PAD - HALO

    @pl.when(t == 0)
    def _():
        ext_ref[0:lo, :] = jnp.zeros((lo, ext_ref.shape[1]), F32)
        ext_ref[lo:HALO_PAD, :] = st_ref[0]

    @pl.when(t > 0)
    def _():
        ext_ref[0:HALO_PAD, :] = ext_ref[tt:tt + HALO_PAD, :]

    ext_ref[HALO_PAD:, :] = u_ref[0]

    taps = [[] for _ in range(SUBLANES)]
    for w in range(CONV_WIDTH):
        taps[(lo + w) % SUBLANES].append(((lo + w) // SUBLANES, w))
    for c0 in range(0, ext_ref.shape[1], lanes):
        cs = slice(c0, c0 + lanes)
        for r0 in range(0, tt, rows):
            acc = None
            for b in range(SUBLANES):
                n = rows + SUBLANES if b else rows
                part = None
                for a, w in taps[b]:
                    start = r0 + a * SUBLANES
                    term = ext_ref[start:start + n, cs] * dw_ref[w:w + 1, cs]
                    part = term if part is None else part + term
                part = part[b:b + rows]
                acc = part + db_ref[:, cs] if acc is None else acc + part
            o_ref[0, r0:r0 + rows, cs] = acc


def _conv(u, state, dw_w, dw_b, *, tt, cols=256):
    b, t, d = u.shape
    return pl.pallas_call(
        functools.partial(_conv_kernel, tt=tt, rows=min(tt, 128), lanes=128),
        grid=(b, d // cols, t // tt),
        in_specs=[
            pl.BlockSpec((1, tt, cols), lambda i, c, j: (i, j, c)),
            pl.BlockSpec((1, HALO, cols), lambda i, c, j: (i, 0, c)),
            pl.BlockSpec((CONV_WIDTH, cols), lambda i, c, j: (0, c)),
            pl.BlockSpec((1, cols), lambda i, c, j: (0, c)),
        ],
        out_specs=pl.BlockSpec((1, tt, cols), lambda i, c, j: (i, j, c)),
        out_shape=jax.ShapeDtypeStruct((b, t, d), F32),
        scratch_shapes=[pltpu.VMEM((HALO_PAD + tt, cols), F32)],
        compiler_params=_params("parallel", "parallel", "arbitrary"),
        name="conv",
    )(u, state, dw_w, dw_b.reshape(1, d))


def _suffix_matrix(n):
    tri = lax.broadcasted_iota(jnp.int32, (n, n), 0) > lax.broadcasted_iota(jnp.int32, (n, n), 1)
    return jnp.concatenate([tri, tri], axis=0).astype(BF16)


def _attn_kernel(q_ref, kd_ref, vd_ref, kp_ref, vp_ref, trid_ref, trip_ref, o_ref, acc_ref, mass_ref,
                 *, g, tq, tk, causal_past, head_rows):
    nt = (((1,), (1,)), ((), ()))
    hd = HEAD_DIM
    qs = [q_ref[0, :, h * hd:(h + 1) * hd] for h in range(g)]

    def block(ks, vs, tri2, mask, first):
        zs, sps = [], []
        for h in range(g):
            z = lax.dot_general(qs[h], ks[h], nt, preferred_element_type=F32)
            neg_abs = lax.bitcast_convert_type(lax.bitcast_convert_type(z, jnp.int32) | SIGN_BIT, F32)
            sp = jnp.maximum(z, 0.0) + jnp.log(1.0 + jnp.exp2(neg_abs)) * LOG2E
            if mask is not None:
                sp = jnp.where(mask, sp, 0.0)
            zs.append(z)
            sps.append(sp)
        z = jnp.concatenate(zs, axis=0)
        sp = jnp.concatenate(sps, axis=0)
        rowsum = jnp.sum(sp, axis=1, keepdims=True)
        hi = sp.astype(BF16)
        lo = (sp - hi.astype(F32)).astype(BF16)
        n = sp.shape[1]
        if n % 128 == 0:
            suffix = jnp.dot(jnp.concatenate([hi, lo], axis=1), tri2, preferred_element_type=F32)
        else:
            suffix = (jnp.dot(hi, tri2[:n], preferred_element_type=F32)
                      + jnp.dot(lo, tri2[n:], preferred_element_type=F32))
        x = z - sp - suffix
        if not first:
            x = x - mass_ref[...]
        a = jnp.exp2(x)
        for h in range(g):
            ah = a[h * tq:(h + 1) * tq]
            if mask is not None:
                ah = jnp.where(mask, ah, 0.0)
            pv = jnp.dot(ah.astype(BF16), vs[h], preferred_element_type=F32)
            if first:
                acc_ref[:, h * hd:(h + 1) * hd] = pv
            else:
                acc_ref[:, h * hd:(h + 1) * hd] += pv
        if first:
            mass_ref[...] = rowsum
        else:
            mass_ref[...] += rowsum

    mask = lax.broadcasted_iota(jnp.int32, (tq, tq), 1) < lax.broadcasted_iota(jnp.int32, (tq, tq), 0)
    block([kd_ref[0, :, h * hd:(h + 1) * hd] for h in range(g)],
          [vd_ref[0, :, h * hd:(h + 1) * hd] for h in range(g)], trid_ref[...], mask, True)

    tri2 = trip_ref[...]
    if head_rows:
        n_past = kp_ref.shape[1] // (g * tk)
        for i in range(n_past):
            start = (n_past - 1 - i) * tk * g

            @pl.when(jnp.min(mass_ref[...]) < EXIT_LOG2)
            def _():
                past = lambda ref, h: ref[0, pl.ds(start + h, tk, stride=g), :].astype(BF16)
                block([past(kp_ref, h) for h in range(g)], [past(vp_ref, h) for h in range(g)], tri2, None, False)
    else:
        n_past = pl.program_id(2) * (tq // tk) if causal_past else kp_ref.shape[1] // tk

        def cond(c):
            i, min_mass = c
            return jnp.logical_and(i < n_past, min_mass < EXIT_LOG2)

        def body(c):
            i, _ = c
            off = pl.multiple_of((n_past - 1 - i) * tk, tk)
            past = lambda ref, h: ref[0, pl.ds(off, tk), h * hd:(h + 1) * hd].astype(BF16)
            block([past(kp_ref, h) for h in range(g)], [past(vp_ref, h) for h in range(g)], tri2, None, False)
            return i + 1, jnp.min(mass_ref[...])

        lax.while_loop(cond, body, (jnp.int32(0), jnp.min(mass_ref[...])))
    o_ref[0] = acc_ref[...].astype(o_ref.dtype)


def _attention(q, k_new, v_new, k_past, v_past, *, g, tq, tk, causal_past):
    b, t, _ = q.shape
    p = k_past.shape[1]
    gw = g * HEAD_DIM
    qspec = pl.BlockSpec((1, tq, gw), lambda bi, h, i: (bi, i, h))
    head_rows = k_past.ndim == 4
    if head_rows:
        assert g == N_HEADS and not causal_past
        k_past, v_past = (a.reshape(b, p * N_HEADS, HEAD_DIM) for a in (k_past, v_past))
        pspec = pl.BlockSpec((1, p * N_HEADS, HEAD_DIM), lambda bi, h, i: (bi, 0, 0))
    else:
        pspec = pl.BlockSpec((1, p, gw), lambda bi, h, i: (bi, 0, h))
    return pl.pallas_call(
        functools.partial(_attn_kernel, g=g, tq=tq, tk=tk, causal_past=causal_past, head_rows=head_rows),
        grid=(b, N_HEADS // g, t // tq),
        in_specs=[qspec, qspec, qspec, pspec, pspec,
                  pl.BlockSpec((2 * tq, tq), lambda bi, h, i: (0, 0)),
                  pl.BlockSpec((2 * tk, tk), lambda bi, h, i: (0, 0))],
        out_specs=qspec,
        out_shape=jax.ShapeDtypeStruct(q.shape, BF16),
        scratch_shapes=[pltpu.VMEM((tq, gw), F32), pltpu.VMEM((g * tq, 1), F32)],
        compiler_params=_params("parallel", "parallel", "arbitrary"),
        name="attention",
    )(q, k_new, v_new, k_past, v_past, _suffix_matrix(tq), _suffix_matrix(tk))


def _trunk(x, conv_state, cache_k, cache_v, p, mw, *, emit, conv_tt, attn_g, tq, tk):
    b, t, d = x.shape
    m = b * t
    hk = N_HEADS * HEAD_DIM
    x = x.reshape(m, d)
    w16 = {}

    def ffn(x, name, layer, final_gain=None):
        gain = p[name + "_norm"]
        if not emit:
            return _ffn(x, gain, *mw[name, layer], layer, final_gain)
        y, *w16[name, layer] = _ffn(x, gain, mw[name + "_w_gate"], mw[name + "_w_up"], mw[name + "_w_down"],
                                    layer, final_gain, tm=m, tf=256, emit=True)
        return y

    def proj(x, name, tn=1024, **kw):
        if not emit:
            return _proj(x, mw[name], tn=tn, **kw)
        n_w = 2 if kw.get("pair") else 1
        *ys, = _proj(x, mw[name], tn=min(tn, 512 // n_w), emit=True, **kw)
        w16[name] = tuple(ys[-n_w:]) if n_w == 2 else ys[-1]
        ys = ys[:-n_w]
        return ys if len(ys) > 1 else ys[0]

    x = ffn(x, "ffn1", 0)
    u = proj(x, "conv_pw1_w", tn=512, gain=p["mix_norm"][0], bias=p["conv_pw1_b"][0], pair="glu")
    u = u.reshape(b, t, d)
    new_state = u[:, t - HALO:]
    c = _conv(u, conv_state, p["conv_dw_w"][0], p["conv_dw_b"][0], tt=conv_tt)
    x = proj(c.reshape(m, d), "conv_pw2_w", gain=p["conv_ln_g"][0], ln_bias=p["conv_ln_b"][0],
             bias=p["conv_pw2_b"][0], residual=x)
    x = ffn(x, "ffn2", 0)

    k_new, k16, v_new, v16 = proj(x, "w_kv", tn=512, gain=p["kv_norm"], pair="split", out_dtypes=(F32, BF16))

    x = ffn(x, "ffn1", 1)
    q = proj(x, "attn_wq", gain=p["mix_norm"][1], out_scale=Q_SCALE, out_dtypes=(BF16,))
    q, k16, v16 = (a.reshape(b, t, hk) for a in (q, k16, v16))
    if cache_k is None:
        o = _attention(q, k16, v16, k16, v16, g=attn_g, tq=tq, tk=tk, causal_past=True)
    else:
        o = _attention(q, k16, v16, cache_k, cache_v, g=attn_g, tq=tq, tk=tk, causal_past=False)
    x = proj(o.reshape(m, hk), "attn_wo", residual=x)
    y = ffn(x, "ffn2", 1, p["final_norm"])
    return (y.reshape(b, t, d), new_state[None], k_new.reshape(b, t, N_HEADS, HEAD_DIM),
            v_new.reshape(b, t, N_HEADS, HEAD_DIM)), w16


def kernel(x_prompt, x_sample, state_conv, cache_k, cache_v, ffn1_norm, ffn1_w_gate, ffn1_w_up, ffn1_w_down, mix_norm, ffn2_norm, ffn2_w_gate, ffn2_w_up, ffn2_w_down, conv_pw1_w, conv_pw1_b, conv_dw_w, conv_dw_b, conv_ln_g, conv_ln_b, conv_pw2_w, conv_pw2_b, kv_norm, w_kv, attn_wq, attn_wo, final_norm):
    p = dict(
        ffn1_norm=ffn1_norm, mix_norm=mix_norm, ffn2_norm=ffn2_norm, kv_norm=kv_norm, final_norm=final_norm,
        conv_pw1_b=conv_pw1_b, conv_dw_w=conv_dw_w, conv_dw_b=conv_dw_b, conv_ln_g=conv_ln_g,
        conv_ln_b=conv_ln_b, conv_pw2_b=conv_pw2_b,
    )
    mw = dict(
        ffn1_w_gate=ffn1_w_gate, ffn1_w_up=ffn1_w_up, ffn1_w_down=ffn1_w_down,
        ffn2_w_gate=ffn2_w_gate, ffn2_w_up=ffn2_w_up, ffn2_w_down=ffn2_w_down,
        conv_pw1_w=conv_pw1_w[0], conv_pw2_w=conv_pw2_w[0], w_kv=w_kv, attn_wq=attn_wq[0], attn_wo=attn_wo[0],
    )
    bp, tp, _ = x_prompt.shape
    bs, ts, _ = x_sample.shape
    (y_s, st_s, k_s, v_s), w16 = _trunk(x_sample, state_conv[0], cache_k, cache_v, p, mw, emit=True,
                                        conv_tt=ts, attn_g=N_HEADS, tq=ts, tk=256)
    zero_state = jnp.zeros((bp, HALO, D_MODEL), F32)
    (y_p, st_p, k_p, v_p), _ = _trunk(x_prompt, zero_state, None, None, p, w16, emit=False,
                                      conv_tt=512, attn_g=4, tq=256, tk=256)
    return (y_p, y_s, st_p, k_p, v_p, st_s, k_s, v_s)
```

```python
import functools

import jax
import jax.numpy as jnp
from jax import lax
from jax.experimental import pallas as pl
from jax.experimental.pallas import tpu as pltpu

D_MODEL = 2048
N_HEADS = 16
HEAD_DIM = 128
CONV_WIDTH = 31
EPS = 1e-6
FFN_RES = 0.5
LOG2E = 1.4426950408889634
Q_SCALE = HEAD_DIM ** -0.5 * LOG2E
EXIT_LOG2 = 150.0
SIGN_BIT = -2 ** 31

SUBLANES = 8
MXU_COLS = 256
HALO = CONV_WIDTH - 1
HALO_PAD = 32

VMEM_LIMIT = 56 * 1024 * 1024

F32 = jnp.float32
BF16 = jnp.bfloat16


def _params(*sem):
    return pltpu.CompilerParams(dimension_semantics=sem, vmem_limit_bytes=VMEM_LIMIT)


def _rms(x, g):
    return x * lax.rsqrt(jnp.mean(x * x, axis=-1, keepdims=True) + EPS) * g


def _ln_silu(c, g, b):
    mu = jnp.mean(c, axis=-1, keepdims=True)
    xc = c - mu
    y = xc * lax.rsqrt(jnp.mean(xc * xc, axis=-1, keepdims=True) + EPS) * g + b
    return y * jax.nn.sigmoid(y)


def _normalize_rows(x_ref, xn_ref, fn, chunk=256):
    chunk = min(chunk, x_ref.shape[0])

    def body(r, carry):
        rows = pl.ds(pl.multiple_of(r * chunk, chunk), chunk)
        xn_ref[rows, :] = fn(x_ref[rows, :]).astype(BF16)
        return carry

    lax.fori_loop(0, x_ref.shape[0] // chunk, body, 0)


def _ffn_kernel(x_ref, g_ref, wg_ref, wu_ref, wd_ref, fg_ref, o_ref, *rest, final, emit):
    w16_refs, xn_ref = rest[:-1], rest[-1]
    f = pl.program_id(1)

    @pl.when(f == 0)
    def _():
        _normalize_rows(x_ref, xn_ref, lambda x: _rms(x, g_ref[...]))
        o_ref[...] = jnp.zeros_like(o_ref)

    ws = [w[...] for w in (wg_ref, wu_ref, wd_ref)]
    if emit:
        ws = [w.astype(BF16) for w in ws]
        for w16_ref, w in zip(w16_refs, ws):
            w16_ref[...] = w
    wg, wu, wd = ws
    xn = xn_ref[...]
    down = None
    tf = wg.shape[1]
    part_w = tf // 2 if tf // 2 >= MXU_COLS else tf
    for c0 in range(0, tf, part_w):
        h = jnp.dot(xn, wg[:, c0:c0 + part_w], preferred_element_type=F32)
        u = jnp.dot(xn, wu[:, c0:c0 + part_w], preferred_element_type=F32)
        a = (h * jax.nn.sigmoid(h) * u).astype(BF16)
        part = jnp.dot(a, wd[c0:c0 + part_w, :], preferred_element_type=F32)
        down = part if down is None else down + part
    o_ref[...] += down

    @pl.when(f == pl.num_programs(1) - 1)
    def _():
        y = x_ref[...] + FFN_RES * o_ref[...]
        if final:
            y = _rms(y, fg_ref[...])
        o_ref[...] = y


def _ffn(x, gain, wg, wu, wd, layer, final_gain=None, *, tm=512, tf=512, emit=False):
    m, d = x.shape
    f = wg.shape[-1]
    final = final_gain is not None
    fg = final_gain if final else gain[layer]
    if emit:
        assert m == tm, "every weight tile must be visited exactly once"
        wspec = lambda r, c, imap: pl.BlockSpec((None, r, c), lambda i, j: (layer,) + imap(j))
    else:
        wspec = lambda r, c, imap: pl.BlockSpec((r, c), lambda i, j: imap(j))
    up_map, down_map = (lambda j: (0, j)), (lambda j: (j, 0))
    once = dict(pipeline_mode=pl.Buffered(1)) if m == tm else {}
    out_specs = [pl.BlockSpec((tm, d), lambda i, j: (i, 0), **once)]
    out_shape = [jax.ShapeDtypeStruct((m, d), F32)]
    if emit:
        out_specs += [pl.BlockSpec((d, tf), lambda i, j: (0, j)), pl.BlockSpec((d, tf), lambda i, j: (0, j)),
                      pl.BlockSpec((tf, d), lambda i, j: (j, 0))]
        out_shape += [jax.ShapeDtypeStruct(s, BF16) for s in ((d, f), (d, f), (f, d))]
    out = pl.pallas_call(
        functools.partial(_ffn_kernel, final=final, emit=emit),
        grid=(m // tm, f // tf),
        in_specs=[
            pl.BlockSpec((tm, d), lambda i, j: (i, 0), **once),
            pl.BlockSpec((1, d), lambda i, j: (0, 0)),
            wspec(d, tf, up_map), wspec(d, tf, up_map), wspec(tf, d, down_map),
            pl.BlockSpec((1, d), lambda i, j: (0, 0)),
        ],
        out_specs=out_specs,
        out_shape=out_shape,
        scratch_shapes=[pltpu.VMEM((tm, d), BF16)],
        compiler_params=_params("parallel", "arbitrary"),
        name="ffn",
    )(x, gain[layer].reshape(1, d), wg, wu, wd, fg.reshape(1, d))
    return out if emit else out[0]


def _proj_kernel(*refs, norm, bias, pair, res, out_scale, n_dt, emit):
    refs = list(refs)
    x_ref = refs.pop(0)
    g_ref = refs.pop(0) if norm else None
    nb_ref = refs.pop(0) if norm == "ln_silu" else None
    n_w = 2 if pair else 1
    w_refs = [refs.pop(0) for _ in range(n_w)]
    b_refs = [refs.pop(0) for _ in range(n_w)] if bias else None
    r_ref = refs.pop(0) if res else None
    n_y = 2 if pair == "split" else 1
    o_refs = [[refs.pop(0) for _ in range(n_dt)] for _ in range(n_y)]
    w16_refs = [refs.pop(0) for _ in range(n_w)] if emit else None
    xn_ref = refs.pop(0) if norm else None

    if norm:
        @pl.when(pl.program_id(1) == 0)
        def _():
            if norm == "rms":
                _normalize_rows(x_ref, xn_ref, lambda x: _rms(x, g_ref[...]))
            else:
                _normalize_rows(x_ref, xn_ref, lambda x: _ln_silu(x, g_ref[...], nb_ref[...]))
        xn = xn_ref[...]
    else:
        xn = x_ref[...]

    ws = [w[...] for w in w_refs]
    if emit:
        ws = [w.astype(BF16) for w in ws]
        for w16_ref, w in zip(w16_refs, ws):
            w16_ref[...] = w
    ys = [jnp.dot(xn, w, preferred_element_type=F32) for w in ws]
    if bias:
        ys = [y + b[...] for y, b in zip(ys, b_refs)]
    if pair == "glu":
        ys = [ys[0] * jax.nn.sigmoid(ys[1])]
    if res:
        ys = [r_ref[...] + ys[0]]
    if out_scale is not None:
        ys = [y * out_scale for y in ys]
    for y, outs in zip(ys, o_refs):
        for o in outs:
            o[...] = y.astype(o.dtype)


def _proj(x, w, *, gain=None, ln_bias=None, bias=None, pair=None, residual=None, out_scale=None,
          out_dtypes=(F32,), tm=1024, tn=1024, emit=False):
    m, d = x.shape
    halves = w if isinstance(w, tuple) else ((w, w) if pair else (w,))
    n = halves[0].shape[1] // (1 if isinstance(w, tuple) or not pair else 2)
    tm, tn = min(tm, m), min(tn, n)
    nb = n // tn
    assert not emit or m == tm, "every weight tile must be visited exactly once"
    norm = None if gain is None else ("rms" if ln_bias is None else "ln_silu")
    once = dict(pipeline_mode=pl.Buffered(1)) if m == tm else {}
    args = [x]
    specs = [pl.BlockSpec((tm, d), lambda i, j: (i, 0), **once)]
    if norm:
        args.append(gain.reshape(1, d))
        specs.append(pl.BlockSpec((1, d), lambda i, j: (0, 0)))
    if norm == "ln_silu":
        args.append(ln_bias.reshape(1, d))
        specs.append(pl.BlockSpec((1, d), lambda i, j: (0, 0)))
    for k, half in enumerate(halves):
        shift = 0 if isinstance(w, tuple) else k * nb
        args.append(half)
        specs.append(pl.BlockSpec((d, tn), lambda i, j, shift=shift: (0, j + shift)))
    if bias is not None:
        b2 = bias.reshape(1, -1)
        args.append(b2)
        specs.append(pl.BlockSpec((1, tn), lambda i, j: (0, j)))
        if pair:
            args.append(b2)
            specs.append(pl.BlockSpec((1, tn), lambda i, j: (0, j + nb)))
    if residual is not None:
        args.append(residual)
        specs.append(pl.BlockSpec((tm, tn), lambda i, j: (i, j)))
    n_out = (2 if pair == "split" else 1) * len(out_dtypes)
    dts = list(out_dtypes) * (n_out // len(out_dtypes))
    out_specs = [pl.BlockSpec((tm, tn), lambda i, j: (i, j)) for _ in dts]
    out_shape = [jax.ShapeDtypeStruct((m, n), dt) for dt in dts]
    if emit:
        out_specs += [pl.BlockSpec((d, tn), lambda i, j: (0, j)) for _ in halves]
        out_shape += [jax.ShapeDtypeStruct((d, n), BF16) for _ in halves]
    out = pl.pallas_call(
        functools.partial(_proj_kernel, norm=norm, bias=bias is not None, pair=pair, res=residual is not None,
                          out_scale=out_scale, n_dt=len(out_dtypes), emit=emit),
        grid=(m // tm, nb),
        in_specs=specs,
        out_specs=out_specs,
        out_shape=out_shape,
        scratch_shapes=[pltpu.VMEM((tm, d), BF16)] if norm else [],
        compiler_params=_params("parallel", "arbitrary"),
        name="proj",
    )(*args)
    return out if len(out) > 1 else out[0]


def _proj_resident_kernel(*refs, norm, bias, res, out_scale, chunk):
    refs = list(refs)
    x_ref = refs.pop(0)
    g_ref = refs.pop(0) if norm else None
    nb_ref = refs.pop(0) if norm == "ln_silu" else None
    w_ref = refs.pop(0)
    b_ref = refs.pop(0) if bias else None
    r_ref = refs.pop(0) if res else None
    o_refs = refs
    for r0 in range(0, x_ref.shape[0], chunk):
        rows = slice(r0, r0 + chunk)
        x = x_ref[rows, :]
        if norm == "rms":
            x = _rms(x, g_ref[...]).astype(BF16)
        elif norm == "ln_silu":
            x = _ln_silu(x, g_ref[...], nb_ref[...]).astype(BF16)
        y = jnp.dot(x, w_ref[...], preferred_element_type=F32)
        if bias:
            y = y + b_ref[...]
        if res:
            y = r_ref[rows, :] + y
        if out_scale is not None:
            y = y * out_scale
        for o in o_refs:
            o[rows, :] = y.astype(o.dtype)


def _proj_resident(x, w, *, gain=None, ln_bias=None, bias=None, residual=None, out_scale=None,
                   out_dtypes=(F32,), tm=512, chunk=256):
    m, d = x.shape
    n = w.shape[1]
    norm = None if gain is None else ("rms" if ln_bias is None else "ln_silu")
    vec = lambda v: (v.reshape(1, -1), pl.BlockSpec((1, v.size), lambda i: (0, 0)))
    args, specs = [x], [pl.BlockSpec((tm, d), lambda i: (i, 0))]
    for v in ([gain] if norm else []) + ([ln_bias] if norm == "ln_silu" else []):
        a, s = vec(v)
        args.append(a)
        specs.append(s)
    args.append(w)
    specs.append(pl.BlockSpec((d, n), lambda i: (0, 0), pipeline_mode=pl.Buffered(1)))
    if bias is not None:
        a, s = vec(bias)
        args.append(a)
        specs.append(s)
    if residual is not None:
        args.append(residual)
        specs.append(pl.BlockSpec((tm, n), lambda i: (i, 0)))
    out = pl.pallas_call(
        functools.partial(_proj_resident_kernel, norm=norm, bias=bias is not None, res=residual is not None,
                          out_scale=out_scale, chunk=chunk),
        grid=(m // tm,),
        in_specs=specs,
        out_specs=[pl.BlockSpec((tm, n), lambda i: (i, 0)) for _ in out_dtypes],
        out_shape=[jax.ShapeDtypeStruct((m, n), dt) for dt in out_dtypes],
        compiler_params=_params("parallel"),
        name="proj_resident",
    )(*args)
    return out if len(out) > 1 else out[0]


def _conv_kernel(u_ref, st_ref, dw_ref, db_ref, o_ref, ext_ref, *, tt, rows, lanes):
    t = pl.program_id(2)
    lo = HALO_PAD - HALO

    @pl.when(t == 0)
    def _():
        ext_ref[0:lo, :] = jnp.zeros((lo, ext_ref.shape[1]), F32)
        ext_ref[lo:HALO_PAD, :] = st_ref[0]

    @pl.when(t > 0)
    def _():
        ext_ref[0:HALO_PAD, :] = ext_ref[tt:tt + HALO_PAD, :]

    ext_ref[HALO_PAD:, :] = u_ref[0]

    taps = [[] for _ in range(SUBLANES)]
    for w in range(CONV_WIDTH):
        taps[(lo + w) % SUBLANES].append(((lo + w) // SUBLANES, w))
    for c0 in range(0, ext_ref.shape[1], lanes):
        cs = slice(c0, c0 + lanes)
        for r0 in range(0, tt, rows):
            acc = None
            for b in range(SUBLANES):
                n = rows + SUBLANES if b else rows
                part = None
                for a, w in taps[b]:
                    start = r0 + a * SUBLANES
                    term = ext_ref[start:start + n, cs] * dw_ref[w:w + 1, cs]
                    part = term if part is None else part + term
                part = part[b:b + rows]
                acc = part + db_ref[:, cs] if acc is None else acc + part
            o_ref[0, r0:r0 + rows, cs] = acc


def _conv(u, state, dw_w, dw_b, *, tt, cols=256):
    b, t, d = u.shape
    return pl.pallas_call(
        functools.partial(_conv_kernel, tt=tt, rows=min(tt, 128), lanes=128),
        grid=(b, d // cols, t // tt),
        in_specs=[
            pl.BlockSpec((1, tt, cols), lambda i, c, j: (i, j, c)),
            pl.BlockSpec((1, HALO, cols), lambda i, c, j: (i, 0, c)),
            pl.BlockSpec((CONV_WIDTH, cols), lambda i, c, j: (0, c)),
            pl.BlockSpec((1, cols), lambda i, c, j: (0, c)),
        ],
        out_specs=pl.BlockSpec((1, tt, cols), lambda i, c, j: (i, j, c)),
        out_shape=jax.ShapeDtypeStruct((b, t, d), F32),
        scratch_shapes=[pltpu.VMEM((HALO_PAD + tt, cols), F32)],
        compiler_params=_params("parallel", "parallel", "arbitrary"),
        name="conv",
    )(u, state, dw_w, dw_b.reshape(1, d))


def _suffix_matrix(n):
    tri = lax.broadcasted_iota(jnp.int32, (n, n), 0) > lax.broadcasted_iota(jnp.int32, (n, n), 1)
    return jnp.concatenate([tri, tri], axis=0).astype(BF16)


def _attn_kernel(q_ref, kd_ref, vd_ref, kp_ref, vp_ref, trid_ref, trip_ref, o_ref, acc_ref, mass_ref,
                 *, g, tq, tk, causal_past, head_rows):
    nt = (((1,), (1,)), ((), ()))
    hd = HEAD_DIM
    qs = [q_ref[0, :, h * hd:(h + 1) * hd] for h in range(g)]

    def block(ks, vs, tri2, mask, first):
        zs, sps = [], []
        for h in range(g):
            z = lax.dot_general(qs[h], ks[h], nt, preferred_element_type=F32)
            neg_abs = lax.bitcast_convert_type(lax.bitcast_convert_type(z, jnp.int32) | SIGN_BIT, F32)
            sp = jnp.maximum(z, 0.0) + jnp.log(1.0 + jnp.exp2(neg_abs)) * LOG2E
            if mask is not None:
                sp = jnp.where(mask, sp, 0.0)
            zs.append(z)
            sps.append(sp)
        z = jnp.concatenate(zs, axis=0)
        sp = jnp.concatenate(sps, axis=0)
        rowsum = jnp.sum(sp, axis=1, keepdims=True)
        hi = sp.astype(BF16)
        lo = (sp - hi.astype(F32)).astype(BF16)
        n = sp.shape[1]
        if n % 128 == 0:
            suffix = jnp.dot(jnp.concatenate([hi, lo], axis=1), tri2, preferred_element_type=F32)
        else:
            suffix = (jnp.dot(hi, tri2[:n], preferred_element_type=F32)
                      + jnp.dot(lo, tri2[n:], preferred_element_type=F32))
        x = z - sp - suffix
        if not first:
            x = x - mass_ref[...]
        a = jnp.exp2(x)
        for h in range(g):
            ah = a[h * tq:(h + 1) * tq]
            if mask is not None:
                ah = jnp.where(mask, ah, 0.0)
            pv = jnp.dot(ah.astype(BF16), vs[h], preferred_element_type=F32)
            if first:
                acc_ref[:, h * hd:(h + 1) * hd] = pv
            else:
                acc_ref[:, h * hd:(h + 1) * hd] += pv
        if first:
            mass_ref[...] = rowsum
        else:
            mass_ref[...] += rowsum

    mask = lax.broadcasted_iota(jnp.int32, (tq, tq), 1) < lax.broadcasted_iota(jnp.int32, (tq, tq), 0)
    block([kd_ref[0, :, h * hd:(h + 1) * hd] for h in range(g)],
          [vd_ref[0, :, h * hd:(h + 1) * hd] for h in range(g)], trid_ref[...], mask, True)

    tri2 = trip_ref[...]
    if head_rows:
        n_past = kp_ref.shape[1] // (g * tk)
        for i in range(n_past):
            start = (n_past - 1 - i) * tk * g

            @pl.when(jnp.min(mass_ref[...]) < EXIT_LOG2)
            def _():
                past = lambda ref, h: ref[0, pl.ds(start + h, tk, stride=g), :].astype(BF16)
                block([past(kp_ref, h) for h in range(g)], [past(vp_ref, h) for h in range(g)], tri2, None, False)
    else:
        n_past = pl.program_id(2) * (tq // tk) if causal_past else kp_ref.shape[1] // tk

        def cond(c):
            i, min_mass = c
            return jnp.logical_and(i < n_past, min_mass < EXIT_LOG2)

        def body(c):
            i, _ = c
            off = pl.multiple_of((n_past - 1 - i) * tk, tk)
            past = lambda ref, h: ref[0, pl.ds(off, tk), h * hd:(h + 1) * hd].astype(BF16)
            block([past(kp_ref, h) for h in range(g)], [past(vp_ref, h) for h in range(g)], tri2, None, False)
            return i + 1, jnp.min(mass_ref[...])

        lax.while_loop(cond, body, (jnp.int32(0), jnp.min(mass_ref[...])))
    o_ref[0] = acc_ref[...].astype(o_ref.dtype)


def _attention(q, k_new, v_new, k_past, v_past, *, g, tq, tk, causal_past):
    b, t, _ = q.shape
    p = k_past.shape[1]
    gw = g * HEAD_DIM
    qspec = pl.BlockSpec((1, tq, gw), lambda bi, h, i: (bi, i, h))
    head_rows = k_past.ndim == 4
    if head_rows:
        assert g == N_HEADS and not causal_past
        k_past, v_past = (a.reshape(b, p * N_HEADS, HEAD_DIM) for a in (k_past, v_past))
        pspec = pl.BlockSpec((1, p * N_HEADS, HEAD_DIM), lambda bi, h, i: (bi, 0, 0))
    else:
        pspec = pl.BlockSpec((1, p, gw), lambda bi, h, i: (bi, 0, h))
    return pl.pallas_call(
        functools.partial(_attn_kernel, g=g, tq=tq, tk=tk, causal_past=causal_past, head_rows=head_rows),
        grid=(b, N_HEADS // g, t // tq),
        in_specs=[qspec, qspec, qspec, pspec, pspec,
                  pl.BlockSpec((2 * tq, tq), lambda bi, h, i: (0, 0)),
                  pl.BlockSpec((2 * tk, tk), lambda bi, h, i: (0, 0))],
        out_specs=qspec,
        out_shape=jax.ShapeDtypeStruct(q.shape, BF16),
        scratch_shapes=[pltpu.VMEM((tq, gw), F32), pltpu.VMEM((g * tq, 1), F32)],
        compiler_params=_params("parallel", "parallel", "arbitrary"),
        name="attention",
    )(q, k_new, v_new, k_past, v_past, _suffix_matrix(tq), _suffix_matrix(tk))


def _trunk(x, conv_state, cache_k, cache_v, p, mw, *, emit, conv_tt, attn_g, tq, tk):
    b, t, d = x.shape
    m = b * t
    hk = N_HEADS * HEAD_DIM
    x = x.reshape(m, d)
    w16 = {}

    def ffn(x, name, layer, final_gain=None):
        gain = p[name + "_norm"]
        if not emit:
            return _ffn(x, gain, *mw[name, layer], layer, final_gain)
        y, *w16[name, layer] = _ffn(x, gain, mw[name + "_w_gate"], mw[name + "_w_up"], mw[name + "_w_down"],
                                    layer, final_gain, tm=m, tf=256, emit=True)
        return y

    def proj(x, name, tn=1024, **kw):
        if not emit:
            if "pair" not in kw:
                return _proj_resident(x, mw[name], **kw)
            return _proj(x, mw[name], tn=tn, **kw)
        n_w = 2 if kw.get("pair") else 1
        *ys, = _proj(x, mw[name], tn=min(tn, 512 // n_w), emit=True, **kw)
        w16[name] = tuple(ys[-n_w:]) if n_w == 2 else ys[-1]
        ys = ys[:-n_w]
        return ys if len(ys) > 1 else ys[0]

    x = ffn(x, "ffn1", 0)
    u = proj(x, "conv_pw1_w", tn=512, gain=p["mix_norm"][0], bias=p["conv_pw1_b"][0], pair="glu")
    u = u.reshape(b, t, d)
    new_state = u[:, t - HALO:]
    c = _conv(u, conv_state, p["conv_dw_w"][0], p["conv_dw_b"][0], tt=conv_tt)
    x = proj(c.reshape(m, d), "conv_pw2_w", gain=p["conv_ln_g"][0], ln_bias=p["conv_ln_b"][0],
             bias=p["conv_pw2_b"][0], residual=x)
    x = ffn(x, "ffn2", 0)

    k_new, k16, v_new, v16 = proj(x, "w_kv", tn=512, gain=p["kv_norm"], pair="split", out_dtypes=(F32, BF16))

    x = ffn(x, "ffn1", 1)
    q = proj(x, "attn_wq", gain=p["mix_norm"][1], out_scale=Q_SCALE, out_dtypes=(BF16,))
    q, k16, v16 = (a.reshape(b, t, hk) for a in (q, k16, v16))
    if cache_k is None:
        o = _attention(q, k16, v16, k16, v16, g=attn_g, tq=tq, tk=tk, causal_past=True)
    else:
        o = _attention(q, k16, v16, cache_k, cache_v, g=attn_g, tq=tq, tk=tk, causal_past=False)
    x = proj(o.reshape(m, hk), "attn_wo", residual=x)
    y = ffn(x, "ffn2", 1, p["final_norm"])
    return (y.reshape(b, t, d), new_state[None], k_new.reshape(b, t, N_HEADS, HEAD_DIM),
            v_new.reshape(b, t, N_HEADS, HEAD_DIM)), w16


def kernel(x_prompt, x_sample, state_conv, cache_k, cache_v, ffn1_norm, ffn1_w_gate, ffn1_w_up, ffn1_w_down, mix_norm, ffn2_norm, ffn2_w_gate, ffn2_w_up, ffn2_w_down, conv_pw1_w, conv_pw1_b, conv_dw_w, conv_dw_b, conv_ln_g, conv_ln_b, conv_pw2_w, conv_pw2_b, kv_norm, w_kv, attn_wq, attn_wo, final_norm):
    p = dict(
        ffn1_norm=ffn1_norm, mix_norm=mix_norm, ffn2_norm=ffn2_norm, kv_norm=kv_norm, final_norm=final_norm,
        conv_pw1_b=conv_pw1_b, conv_dw_w=conv_dw_w, conv_dw_b=conv_dw_b, conv_ln_g=conv_ln_g,
        conv_ln_b=conv_ln_b, conv_pw2_b=conv_pw2_b,
    )
    mw = dict(
        ffn1_w_gate=ffn1_w_gate, ffn1_w_up=ffn1_w_up, ffn1_w_down=ffn1_w_down,
        ffn2_w_gate=ffn2_w_gate, ffn2_w_up=ffn2_w_up, ffn2_w_down=ffn2_w_down,
        conv_pw1_w=conv_pw1_w[0], conv_pw2_w=conv_pw2_w[0], w_kv=w_kv, attn_wq=attn_wq[0], attn_wo=attn_wo[0],
    )
    bp, tp, _ = x_prompt.shape
    bs, ts, _ = x_sample.shape
    (y_s, st_s, k_s, v_s), w16 = _trunk(x_sample, state_conv[0], cache_k, cache_v, p, mw, emit=True,
                                        conv_tt=ts, attn_g=N_HEADS, tq=ts, tk=256)
    x_prompt, y_s = lax.optimization_barrier((x_prompt, y_s))
    zero_state = jnp.zeros((bp, HALO, D_MODEL), F32)
    (y_p, st_p, k_p, v_p), _ = _trunk(x_prompt, zero_state, None, None, p, w16, emit=False,
                                      conv_tt=512, attn_g=4, tq=256, tk=256)
    return (y_p, y_s, st_p, k_p, v_p, st_s, k_s, v_s)
```

```python
import functools

import jax
import jax.numpy as jnp
from jax import lax
from jax.experimental import pallas as pl
from jax.experimental.pallas import tpu as pltpu

D_MODEL = 2048
N_HEADS = 16
HEAD_DIM = 128
CONV_WIDTH = 31
EPS = 1e-6
FFN_RES = 0.5
LOG2E = 1.4426950408889634
Q_SCALE = HEAD_DIM ** -0.5 * LOG2E
EXIT_LOG2 = 150.0
SIGN_BIT = -2 ** 31
NEAR_BLOCKS = 2

SUBLANES = 8
MXU_COLS = 256
HALO = CONV_WIDTH - 1
HALO_PAD = 32
CONV_BLOCK_ELEMS = 256 * 1024

VMEM_LIMIT = 56 * 1024 * 1024
RESIDENT_BUDGET = VMEM_LIMIT - 20 * 1024 * 1024

F32 = jnp.float32
BF16 = jnp.bfloat16


def _params(*sem):
    return pltpu.CompilerParams(dimension_semantics=sem, vmem_limit_bytes=VMEM_LIMIT)


def _rms(x, g):
    return x * lax.rsqrt(jnp.mean(x * x, axis=-1, keepdims=True) + EPS) * g


def _ln_silu(c, g, b):
    mu = jnp.mean(c, axis=-1, keepdims=True)
    xc = c - mu
    y = xc * lax.rsqrt(jnp.mean(xc * xc, axis=-1, keepdims=True) + EPS) * g + b
    return y * jax.nn.sigmoid(y)


def _normalize_rows(x_ref, xn_ref, fn, chunk=256):
    chunk = min(chunk, x_ref.shape[0])

    def body(r, carry):
        rows = pl.ds(pl.multiple_of(r * chunk, chunk), chunk)
        xn_ref[rows, :] = fn(x_ref[rows, :]).astype(BF16)
        return carry

    lax.fori_loop(0, x_ref.shape[0] // chunk, body, 0)


def _ffn_kernel(x_ref, g_ref, wg_ref, wu_ref, wd_ref, fg_ref, o_ref, *rest, final, emit):
    w16_refs, xn_ref = rest[:-1], rest[-1]
    f = pl.program_id(1)

    @pl.when(f == 0)
    def _():
        _normalize_rows(x_ref, xn_ref, lambda x: _rms(x, g_ref[...]))
        o_ref[...] = jnp.zeros_like(o_ref)

    ws = [w[...] for w in (wg_ref, wu_ref, wd_ref)]
    if emit:
        ws = [w.astype(BF16) for w in ws]
        for w16_ref, w in zip(w16_refs, ws):
            w16_ref[...] = w
    wg, wu, wd = ws
    xn = xn_ref[...]
    down = None
    tf = wg.shape[1]
    part_w = tf // 2 if tf // 2 >= MXU_COLS else tf
    for c0 in range(0, tf, part_w):
        h = jnp.dot(xn, wg[:, c0:c0 + part_w], preferred_element_type=F32)
        u = jnp.dot(xn, wu[:, c0:c0 + part_w], preferred_element_type=F32)
        a = (h * jax.nn.sigmoid(h) * u).astype(BF16)
        part = jnp.dot(a, wd[c0:c0 + part_w, :], preferred_element_type=F32)
        down = part if down is None else down + part
    o_ref[...] += down

    @pl.when(f == pl.num_programs(1) - 1)
    def _():
        y = x_ref[...] + FFN_RES * o_ref[...]
        if final:
            y = _rms(y, fg_ref[...])
        o_ref[...] = y


def _ffn(x, gain, wg, wu, wd, layer, final_gain=None, *, tm=512, tf=512, emit=False):
    m, d = x.shape
    f = wg.shape[-1]
    final = final_gain is not None
    fg = final_gain if final else gain[layer]
    if emit:
        assert m == tm, "every weight tile must be visited exactly once"
        wspec = lambda r, c, imap: pl.BlockSpec((None, r, c), lambda i, j: (layer,) + imap(j))
    else:
        wspec = lambda r, c, imap: pl.BlockSpec((r, c), lambda i, j: imap(j))
    up_map, down_map = (lambda j: (0, j)), (lambda j: (j, 0))
    once = dict(pipeline_mode=pl.Buffered(1)) if m == tm else {}
    out_specs = [pl.BlockSpec((tm, d), lambda i, j: (i, 0), **once)]
    out_shape = [jax.ShapeDtypeStruct((m, d), F32)]
    if emit:
        out_specs += [pl.BlockSpec((d, tf), lambda i, j: (0, j)), pl.BlockSpec((d, tf), lambda i, j: (0, j)),
                      pl.BlockSpec((tf, d), lambda i, j: (j, 0))]
        out_shape += [jax.ShapeDtypeStruct(s, BF16) for s in ((d, f), (d, f), (f, d))]
    out = pl.pallas_call(
        functools.partial(_ffn_kernel, final=final, emit=emit),
        grid=(m // tm, f // tf),
        in_specs=[
            pl.BlockSpec((tm, d), lambda i, j: (i, 0), **once),
            pl.BlockSpec((1, d), lambda i, j: (0, 0)),
            wspec(d, tf, up_map), wspec(d, tf, up_map), wspec(tf, d, down_map),
            pl.BlockSpec((1, d), lambda i, j: (0, 0)),
        ],
        out_specs=out_specs,
        out_shape=out_shape,
        scratch_shapes=[pltpu.VMEM((tm, d), BF16)],
        compiler_params=_params("parallel", "arbitrary"),
        name="ffn",
    )(x, gain[layer].reshape(1, d), wg, wu, wd, fg.reshape(1, d))
    return out if emit else out[0]


def _proj_kernel(*refs, norm, bias, pair, res, out_scale, n_dt, emit):
    refs = list(refs)
    x_ref = refs.pop(0)
    g_ref = refs.pop(0) if norm else None
    nb_ref = refs.pop(0) if norm == "ln_silu" else None
    n_w = 2 if pair else 1
    w_refs = [refs.pop(0) for _ in range(n_w)]
    b_refs = [refs.pop(0) for _ in range(n_w)] if bias else None
    r_ref = refs.pop(0) if res else None
    n_y = 2 if pair == "split" else 1
    o_refs = [[refs.pop(0) for _ in range(n_dt)] for _ in range(n_y)]
    w16_refs = [refs.pop(0) for _ in range(n_w)] if emit else None
    xn_ref = refs.pop(0) if norm else None

    if norm:
        @pl.when(pl.program_id(1) == 0)
        def _():
            if norm == "rms":
                _normalize_rows(x_ref, xn_ref, lambda x: _rms(x, g_ref[...]))
            else:
                _normalize_rows(x_ref, xn_ref, lambda x: _ln_silu(x, g_ref[...], nb_ref[...]))
        xn = xn_ref[...]
    else:
        xn = x_ref[...]

    ws = [w[...] for w in w_refs]
    if emit:
        ws = [w.astype(BF16) for w in ws]
        for w16_ref, w in zip(w16_refs, ws):
            w16_ref[...] = w
    ys = [jnp.dot(xn, w, preferred_element_type=F32) for w in ws]
    if bias:
        ys = [y + b[...] for y, b in zip(ys, b_refs)]
    if pair == "glu":
        ys = [ys[0] * jax.nn.sigmoid(ys[1])]
    if res:
        ys = [r_ref[...] + ys[0]]
    if out_scale is not None:
        ys = [y * out_scale for y in ys]
    for y, outs in zip(ys, o_refs):
        for o in outs:
            o[...] = y.astype(o.dtype)


def _proj(x, w, *, gain=None, ln_bias=None, bias=None, pair=None, residual=None, out_scale=None,
          out_dtypes=(F32,), tm=1024, tn=1024, emit=False):
    m, d = x.shape
    halves = w if isinstance(w, tuple) else ((w, w) if pair else (w,))
    n = halves[0].shape[1] // (1 if isinstance(w, tuple) or not pair else 2)
    tm, tn = min(tm, m), min(tn, n)
    nb = n // tn
    assert not emit or m == tm, "every weight tile must be visited exactly once"
    norm = None if gain is None else ("rms" if ln_bias is None else "ln_silu")
    once = dict(pipeline_mode=pl.Buffered(1)) if m == tm else {}
    args = [x]
    specs = [pl.BlockSpec((tm, d), lambda i, j: (i, 0), **once)]
    if norm:
        args.append(gain.reshape(1, d))
        specs.append(pl.BlockSpec((1, d), lambda i, j: (0, 0)))
    if norm == "ln_silu":
        args.append(ln_bias.reshape(1, d))
        specs.append(pl.BlockSpec((1, d), lambda i, j: (0, 0)))
    for k, half in enumerate(halves):
        shift = 0 if isinstance(w, tuple) else k * nb
        args.append(half)
        specs.append(pl.BlockSpec((d, tn), lambda i, j, shift=shift: (0, j + shift)))
    if bias is not None:
        b2 = bias.reshape(1, -1)
        args.append(b2)
        specs.append(pl.BlockSpec((1, tn), lambda i, j: (0, j)))
        if pair:
            args.append(b2)
            specs.append(pl.BlockSpec((1, tn), lambda i, j: (0, j + nb)))
    if residual is not None:
        args.append(residual)
        specs.append(pl.BlockSpec((tm, tn), lambda i, j: (i, j)))
    n_out = (2 if pair == "split" else 1) * len(out_dtypes)
    dts = list(out_dtypes) * (n_out // len(out_dtypes))
    out_specs = [pl.BlockSpec((tm, tn), lambda i, j: (i, j)) for _ in dts]
    out_shape = [jax.ShapeDtypeStruct((m, n), dt) for dt in dts]
    if emit:
        out_specs += [pl.BlockSpec((d, tn), lambda i, j: (0, j)) for _ in halves]
        out_shape += [jax.ShapeDtypeStruct((d, n), BF16) for _ in halves]
    out = pl.pallas_call(
        functools.partial(_proj_kernel, norm=norm, bias=bias is not None, pair=pair, res=residual is not None,
                          out_scale=out_scale, n_dt=len(out_dtypes), emit=emit),
        grid=(m // tm, nb),
        in_specs=specs,
        out_specs=out_specs,
        out_shape=out_shape,
        scratch_shapes=[pltpu.VMEM((tm, d), BF16)] if norm else [],
        compiler_params=_params("parallel", "arbitrary"),
        name="proj",
    )(*args)
    return out if len(out) > 1 else out[0]


def _proj_resident_kernel(*refs, norm, bias, pair, res, out_scale, n_dt, chunk):
    refs = list(refs)
    x_ref = refs.pop(0)
    g_ref = refs.pop(0) if norm else None
    nb_ref = refs.pop(0) if norm == "ln_silu" else None
    n_w = 2 if pair else 1
    w_refs = [refs.pop(0) for _ in range(n_w)]
    b_ref = refs.pop(0) if bias else None
    r_ref = refs.pop(0) if res else None
    o_refs = [refs[k * n_dt:(k + 1) * n_dt] for k in range(2 if pair == "split" else 1)]
    n = w_refs[0].shape[1]
    for r0 in range(0, x_ref.shape[0], chunk):
        rows = slice(r0, r0 + chunk)
        x = x_ref[rows, :]
        if norm == "rms":
            x = _rms(x, g_ref[...]).astype(BF16)
        elif norm == "ln_silu":
            x = _ln_silu(x, g_ref[...], nb_ref[...]).astype(BF16)
        ys = [jnp.dot(x, w[...], preferred_element_type=F32) for w in w_refs]
        if bias:
            ys = [y + b_ref[:, k * n:(k + 1) * n] for k, y in enumerate(ys)]
        if pair == "glu":
            ys = [ys[0] * jax.nn.sigmoid(ys[1])]
        if res:
            ys = [r_ref[rows, :] + ys[0]]
        if out_scale is not None:
            ys = [y * out_scale for y in ys]
        for y, outs in zip(ys, o_refs):
            for o in outs:
                o[rows, :] = y.astype(o.dtype)


def _proj_resident(x, w, *, gain=None, ln_bias=None, bias=None, pair=None, residual=None, out_scale=None,
                   out_dtypes=(F32,), tm=None, chunk=256):
    m, d = x.shape
    ws = w if pair else (w,)
    n = ws[0].shape[1]
    norm = None if gain is None else ("rms" if ln_bias is None else "ln_silu")
    dts = list(out_dtypes) * (2 if pair == "split" else 1)
    if tm is None:
        weight_bytes = sum(wk.size * wk.dtype.itemsize for wk in ws)
        row_bytes = d * x.dtype.itemsize + n * sum(jnp.dtype(dt).itemsize for dt in dts)
        row_bytes += 0 if residual is None else n * residual.dtype.itemsize
        tm = next(t for t in (512, 256) if weight_bytes + 2 * t * row_bytes <= RESIDENT_BUDGET)
    chunk = min(chunk, tm)
    vec = lambda v: (v.reshape(1, -1), pl.BlockSpec((1, v.size), lambda i: (0, 0)))
    args, specs = [x], [pl.BlockSpec((tm, d), lambda i: (i, 0))]
    for v in ([gain] if norm else []) + ([ln_bias] if norm == "ln_silu" else []):
        a, s = vec(v)
        args.append(a)
        specs.append(s)
    for wk in ws:
        args.append(wk)
        specs.append(pl.BlockSpec((d, n), lambda i: (0, 0), pipeline_mode=pl.Buffered(1)))
    if bias is not None:
        a, s = vec(bias)
        args.append(a)
        specs.append(s)
    if residual is not None:
        args.append(residual)
        specs.append(pl.BlockSpec((tm, n), lambda i: (i, 0)))
    out = pl.pallas_call(
        functools.partial(_proj_resident_kernel, norm=norm, bias=bias is not None, pair=pair,
                          res=residual is not None, out_scale=out_scale, n_dt=len(out_dtypes), chunk=chunk),
        grid=(m // tm,),
        in_specs=specs,
        out_specs=[pl.BlockSpec((tm, n), lambda i: (i, 0)) for _ in dts],
        out_shape=[jax.ShapeDtypeStruct((m, n), dt) for dt in dts],
        compiler_params=_params("parallel"),
        name="proj_resident",
    )(*args)
    return out if len(out) > 1 else out[0]


def _conv_kernel(u_ref, st_ref, dw_ref, db_ref, o_ref, ext_ref, *, tt, rows, lanes):
    t = pl.program_id(2)
    lo = HALO_PAD - HALO

    @pl.when(t == 0)
    def _():
        ext_ref[0:lo, :] = jnp.zeros((lo, ext_ref.shape[1]), F32)
        ext_ref[lo:HALO_PAD, :] = st_ref[0]

    @pl.when(t > 0)
    def _():
        ext_ref[0:HALO_PAD, :] = ext_ref[tt:tt + HALO_PAD, :]

    ext_ref[HALO_PAD:, :] = u_ref[0]

    taps = [[] for _ in range(SUBLANES)]
    for w in range(CONV_WIDTH):
        taps[(lo + w) % SUBLANES].append(((lo + w) // SUBLANES, w))
    for c0 in range(0, ext_ref.shape[1], lanes):
        cs = slice(c0, c0 + lanes)
        for r0 in range(0, tt, rows):
            acc = None
            for b in range(SUBLANES):
                n = rows + SUBLANES if b else rows
                part = None
                for a, w in taps[b]:
                    start = r0 + a * SUBLANES
                    term = ext_ref[start:start + n, cs] * dw_ref[w:w + 1, cs]
                    part = term if part is None else part + term
                part = part[b:b + rows]
                acc = part + db_ref[:, cs] if acc is None else acc + part
            o_ref[0, r0:r0 + rows, cs] = acc


def _conv(u, state, dw_w, dw_b, *, tt):
    b, t, d = u.shape
    cols = min(d, CONV_BLOCK_ELEMS // tt)
    return pl.pallas_call(
        functools.partial(_conv_kernel, tt=tt, rows=min(tt, 128), lanes=128),
        grid=(b, d // cols, t // tt),
        in_specs=[
            pl.BlockSpec((1, tt, cols), lambda i, c, j: (i, j, c)),
            pl.BlockSpec((1, HALO, cols), lambda i, c, j: (i, 0, c)),
            pl.BlockSpec((CONV_WIDTH, cols), lambda i, c, j: (0, c)),
            pl.BlockSpec((1, cols), lambda i, c, j: (0, c)),
        ],
        out_specs=pl.BlockSpec((1, tt, cols), lambda i, c, j: (i, j, c)),
        out_shape=jax.ShapeDtypeStruct((b, t, d), F32),
        scratch_shapes=[pltpu.VMEM((HALO_PAD + tt, cols), F32)],
        compiler_params=_params("parallel", "parallel", "arbitrary"),
        name="conv",
    )(u, state, dw_w, dw_b.reshape(1, d))


def _suffix_matrix(n):
    tri = lax.broadcasted_iota(jnp.int32, (n, n), 0) > lax.broadcasted_iota(jnp.int32, (n, n), 1)
    return jnp.concatenate([tri, tri], axis=0).astype(BF16)


def _attn_kernel(q_ref, kd_ref, vd_ref, kp_ref, vp_ref, *rest, g, tq, tk, causal_past, head_rows, n_far):
    if n_far:
        k_hbm, v_hbm, trid_ref, trip_ref, o_ref, acc_ref, mass_ref, kfar_ref, vfar_ref, far_sem = rest
    else:
        trid_ref, trip_ref, o_ref, acc_ref, mass_ref = rest
    nt = (((1,), (1,)), ((), ()))
    hd = HEAD_DIM
    qs = [q_ref[0, :, h * hd:(h + 1) * hd] for h in range(g)]

    def block(ks, vs, tri2, mask, first):
        zs, sps = [], []
        for h in range(g):
            z = lax.dot_general(qs[h], ks[h], nt, preferred_element_type=F32)
            neg_abs = lax.bitcast_convert_type(lax.bitcast_convert_type(z, jnp.int32) | SIGN_BIT, F32)
            sp = jnp.maximum(z, 0.0) + jnp.log(1.0 + jnp.exp2(neg_abs)) * LOG2E
            if mask is not None:
                sp = jnp.where(mask, sp, 0.0)
            zs.append(z)
            sps.append(sp)
        z = jnp.concatenate(zs, axis=0)
        sp = jnp.concatenate(sps, axis=0)
        rowsum = jnp.sum(sp, axis=1, keepdims=True)
        hi = sp.astype(BF16)
        lo = (sp - hi.astype(F32)).astype(BF16)
        n = sp.shape[1]
        if n % 128 == 0:
            suffix = jnp.dot(jnp.concatenate([hi, lo], axis=1), tri2, preferred_element_type=F32)
        else:
            suffix = (jnp.dot(hi, tri2[:n], preferred_element_type=F32)
                      + jnp.dot(lo, tri2[n:], preferred_element_type=F32))
        x = z - sp - suffix
        if not first:
            x = x - mass_ref[...]
        a = jnp.exp2(x)
        for h in range(g):
            ah = a[h * tq:(h + 1) * tq]
            if mask is not None:
                ah = jnp.where(mask, ah, 0.0)
            pv = jnp.dot(ah.astype(BF16), vs[h], preferred_element_type=F32)
            if first:
                acc_ref[:, h * hd:(h + 1) * hd] = pv
            else:
                acc_ref[:, h * hd:(h + 1) * hd] += pv
        if first:
            mass_ref[...] = rowsum
        else:
            mass_ref[...] += rowsum

    mask = lax.broadcasted_iota(jnp.int32, (tq, tq), 1) < lax.broadcasted_iota(jnp.int32, (tq, tq), 0)
    block([kd_ref[0, :, h * hd:(h + 1) * hd] for h in range(g)],
          [vd_ref[0, :, h * hd:(h + 1) * hd] for h in range(g)], trid_ref[...], mask, True)

    tri2 = trip_ref[...]
    if head_rows:
        n_near = kp_ref.shape[1] // (g * tk)
        for i in range(n_near + n_far):

            @pl.when(jnp.min(mass_ref[...]) < EXIT_LOG2)
            def _():
                if i < n_near:
                    kr, vr, start = kp_ref.at[0], vp_ref.at[0], (n_near - 1 - i) * tk * g
                else:
                    src = pl.ds((n_far - 1 - (i - n_near)) * tk * g, tk * g)
                    copies = [pltpu.make_async_copy(hbm.at[pl.program_id(0), src, :], buf, far_sem.at[j])
                              for j, (hbm, buf) in enumerate(((k_hbm, kfar_ref), (v_hbm, vfar_ref)))]
                    for c in copies:
                        c.start()
                    for c in copies:
                        c.wait()
                    kr, vr, start = kfar_ref, vfar_ref, 0
                past = lambda ref, h: ref[pl.ds(start + h, tk, stride=g), :].astype(BF16)
                block([past(kr, h) for h in range(g)], [past(vr, h) for h in range(g)], tri2, None, False)
    else:
        n_past = pl.program_id(2) * (tq // tk) if causal_past else kp_ref.shape[1] // tk

        def cond(c):
            i, min_mass = c
            return jnp.logical_and(i < n_past, min_mass < EXIT_LOG2)

        def body(c):
            i, _ = c
            off = pl.multiple_of((n_past - 1 - i) * tk, tk)
            past = lambda ref, h: ref[0, pl.ds(off, tk), h * hd:(h + 1) * hd].astype(BF16)
            block([past(kp_ref, h) for h in range(g)], [past(vp_ref, h) for h in range(g)], tri2, None, False)
            return i + 1, jnp.min(mass_ref[...])

        lax.while_loop(cond, body, (jnp.int32(0), jnp.min(mass_ref[...])))
    o_ref[0] = acc_ref[...].astype(o_ref.dtype)


def _attention(q, k_new, v_new, k_past, v_past, *, g, tq, tk, causal_past):
    b, t, _ = q.shape
    p = k_past.shape[1]
    gw = g * HEAD_DIM
    qspec = pl.BlockSpec((1, tq, gw), lambda bi, h, i: (bi, i, h))
    head_rows = k_past.ndim == 4
    past_args, past_specs, far_scratch, n_far = [k_past, v_past], [], [], 0
    if head_rows:
        assert g == N_HEADS and not causal_past
        k_past, v_past = (a.reshape(b, p * N_HEADS, HEAD_DIM) for a in (k_past, v_past))
        near = min(NEAR_BLOCKS * tk, p)
        n_far = (p - near) // tk
        near_spec = pl.BlockSpec((1, near * N_HEADS, HEAD_DIM), lambda bi, h, i: (bi, p // near - 1, 0))
        past_args, past_specs = [k_past, v_past], [near_spec, near_spec]
        if n_far:
            past_args += [k_past, v_past]
            past_specs += [pl.BlockSpec(memory_space=pl.ANY)] * 2
            far_scratch = [pltpu.VMEM((tk * N_HEADS, HEAD_DIM), k_past.dtype)] * 2 + [pltpu.SemaphoreType.DMA((2,))]
    else:
        past_specs = [pl.BlockSpec((1, p, gw), lambda bi, h, i: (bi, 0, h))] * 2
    return pl.pallas_call(
        functools.partial(_attn_kernel, g=g, tq=tq, tk=tk, causal_past=causal_past, head_rows=head_rows,
                          n_far=n_far),
        grid=(b, N_HEADS // g, t // tq),
        in_specs=[qspec, qspec, qspec, *past_specs,
                  pl.BlockSpec((2 * tq, tq), lambda bi, h, i: (0, 0)),
                  pl.BlockSpec((2 * tk, tk), lambda bi, h, i: (0, 0))],
        out_specs=qspec,
        out_shape=jax.ShapeDtypeStruct(q.shape, BF16),
        scratch_shapes=[pltpu.VMEM((tq, gw), F32), pltpu.VMEM((g * tq, 1), F32), *far_scratch],
        compiler_params=_params("parallel", "parallel", "arbitrary"),
        name="attention",
    )(q, k_new, v_new, *past_args, _suffix_matrix(tq), _suffix_matrix(tk))


def _trunk(x, conv_state, cache_k, cache_v, p, mw, *, emit, conv_tt, attn_g, tq, tk):
    b, t, d = x.shape
    m = b * t
    hk = N_HEADS * HEAD_DIM
    x = x.reshape(m, d)
    w16 = {}

    def ffn(x, name, layer, final_gain=None):
        gain = p[name + "_norm"]
        if not emit:
            return _ffn(x, gain, *mw[name, layer], layer, final_gain)
        y, *w16[name, layer] = _ffn(x, gain, mw[name + "_w_gate"], mw[name + "_w_up"], mw[name + "_w_down"],
                                    layer, final_gain, tm=m, tf=256, emit=True)
        return y

    def proj(x, name, tn=1024, **kw):
        if not emit:
            return _proj_resident(x, mw[name], **kw)
        n_w = 2 if kw.get("pair") else 1
        *ys, = _proj(x, mw[name], tn=min(tn, 512 // n_w), emit=True, **kw)
        w16[name] = tuple(ys[-n_w:]) if n_w == 2 else ys[-1]
        ys = ys[:-n_w]
        return ys if len(ys) > 1 else ys[0]

    x = ffn(x, "ffn1", 0)
    u = proj(x, "conv_pw1_w", tn=512, gain=p["mix_norm"][0], bias=p["conv_pw1_b"][0], pair="glu")
    u = u.reshape(b, t, d)
    new_state = u[:, t - HALO:]
    c = _conv(u, conv_state, p["conv_dw_w"][0], p["conv_dw_b"][0], tt=conv_tt)
    x = proj(c.reshape(m, d), "conv_pw2_w", gain=p["conv_ln_g"][0], ln_bias=p["conv_ln_b"][0],
             bias=p["conv_pw2_b"][0], residual=x)
    x = ffn(x, "ffn2", 0)

    k_new, k16, v_new, v16 = proj(x, "w_kv", tn=512, gain=p["kv_norm"], pair="split", out_dtypes=(F32, BF16))

    x = ffn(x, "ffn1", 1)
    q = proj(x, "attn_wq", gain=p["mix_norm"][1], out_scale=Q_SCALE, out_dtypes=(BF16,))
    k_new, v_new, q = lax.optimization_barrier((k_new, v_new, q))
    q, k16, v16 = (a.reshape(b, t, hk) for a in (q, k16, v16))
    if cache_k is None:
        o = _attention(q, k16, v16, k16, v16, g=attn_g, tq=tq, tk=tk, causal_past=True)
    else:
        o = _attention(q, k16, v16, cache_k, cache_v, g=attn_g, tq=tq, tk=tk, causal_past=False)
    x = proj(o.reshape(m, hk), "attn_wo", residual=x)
    y = ffn(x, "ffn2", 1, p["final_norm"])
    return (y.reshape(b, t, d), new_state[None], k_new.reshape(b, t, N_HEADS, HEAD_DIM),
            v_new.reshape(b, t, N_HEADS, HEAD_DIM)), w16


def kernel(x_prompt, x_sample, state_conv, cache_k, cache_v, ffn1_norm, ffn1_w_gate, ffn1_w_up, ffn1_w_down, mix_norm, ffn2_norm, ffn2_w_gate, ffn2_w_up, ffn2_w_down, conv_pw1_w, conv_pw1_b, conv_dw_w, conv_dw_b, conv_ln_g, conv_ln_b, conv_pw2_w, conv_pw2_b, kv_norm, w_kv, attn_wq, attn_wo, final_norm):
    p = dict(
        ffn1_norm=ffn1_norm, mix_norm=mix_norm, ffn2_norm=ffn2_norm, kv_norm=kv_norm, final_norm=final_norm,
        conv_pw1_b=conv_pw1_b, conv_dw_w=conv_dw_w, conv_dw_b=conv_dw_b, conv_ln_g=conv_ln_g,
        conv_ln_b=conv_ln_b, conv_pw2_b=conv_pw2_b,
    )
    mw = dict(
        ffn1_w_gate=ffn1_w_gate, ffn1_w_up=ffn1_w_up, ffn1_w_down=ffn1_w_down,
        ffn2_w_gate=ffn2_w_gate, ffn2_w_up=ffn2_w_up, ffn2_w_down=ffn2_w_down,
        conv_pw1_w=conv_pw1_w[0], conv_pw2_w=conv_pw2_w[0], w_kv=w_kv, attn_wq=attn_wq[0], attn_wo=attn_wo[0],
    )
    bp, tp, _ = x_prompt.shape
    bs, ts, _ = x_sample.shape
    (y_s, st_s, k_s, v_s), w16 = _trunk(x_sample, state_conv[0], cache_k, cache_v, p, mw, emit=True,
                                        conv_tt=ts, attn_g=N_HEADS, tq=ts, tk=256)
    x_prompt, y_s = lax.optimization_barrier((x_prompt, y_s))
    zero_state = jnp.zeros((bp, HALO, D_MODEL), F32)
    (y_p, st_p, k_p, v_p), _ = _trunk(x_prompt, zero_state, None, None, p, w16, emit=False,
                                      conv_tt=512, attn_g=4, tq=256, tk=256)
    return (y_p, y_s, st_p, k_p, v_p, st_s, k_s, v_s)
```

```python
import functools

import jax
import jax.numpy as jnp
from jax import lax
from jax.experimental import pallas as pl
from jax.experimental.pallas import tpu as pltpu

D_MODEL = 2048
N_HEADS = 16
HEAD_DIM = 128
CONV_WIDTH = 31
EPS = 1e-6
FFN_RES = 0.5
LOG2E = 1.4426950408889634
Q_SCALE = HEAD_DIM ** -0.5 * LOG2E
EXIT_LOG2 = 150.0
SIGN_BIT = -2 ** 31
NEAR_BLOCKS = 2

SUBLANES = 8
MXU_COLS = 256
HALO = CONV_WIDTH - 1
HALO_PAD = 32
CONV_BLOCK_ELEMS = 256 * 1024

VMEM_LIMIT = 56 * 1024 * 1024
RESIDENT_BUDGET = VMEM_LIMIT - 20 * 1024 * 1024

F32 = jnp.float32
BF16 = jnp.bfloat16


def _params(*sem):
    return pltpu.CompilerParams(dimension_semantics=sem, vmem_limit_bytes=VMEM_LIMIT)


def _rms(x, g):
    return x * lax.rsqrt(jnp.mean(x * x, axis=-1, keepdims=True) + EPS) * g


def _ln_silu(c, g, b):
    mu = jnp.mean(c, axis=-1, keepdims=True)
    xc = c - mu
    y = xc * lax.rsqrt(jnp.mean(xc * xc, axis=-1, keepdims=True) + EPS) * g + b
    return y * jax.nn.sigmoid(y)


def _normalize_rows(x_ref, xn_ref, fn, chunk=256):
    chunk = min(chunk, x_ref.shape[0])

    def body(r, carry):
        rows = pl.ds(pl.multiple_of(r * chunk, chunk), chunk)
        xn_ref[rows, :] = fn(x_ref[rows, :]).astype(BF16)
        return carry

    lax.fori_loop(0, x_ref.shape[0] // chunk, body, 0)


def _ffn_kernel(x_ref, g_ref, wg_ref, wu_ref, wd_ref, fg_ref, o_ref, *rest, final, emit):
    w16_refs, xn_ref = rest[:-1], rest[-1]
    f = pl.program_id(1)

    @pl.when(f == 0)
    def _():
        _normalize_rows(x_ref, xn_ref, lambda x: _rms(x, g_ref[...]))
        o_ref[...] = jnp.zeros_like(o_ref)

    ws = [w[...] for w in (wg_ref, wu_ref, wd_ref)]
    if emit:
        ws = [w.astype(BF16) for w in ws]
        for w16_ref, w in zip(w16_refs, ws):
            w16_ref[...] = w
    wg, wu, wd = ws
    xn = xn_ref[...]
    down = None
    tf = wg.shape[1]
    part_w = tf // 2 if tf // 2 >= MXU_COLS else tf
    for c0 in range(0, tf, part_w):
        h = jnp.dot(xn, wg[:, c0:c0 + part_w], preferred_element_type=F32)
        u = jnp.dot(xn, wu[:, c0:c0 + part_w], preferred_element_type=F32)
        a = (h * jax.nn.sigmoid(h) * u).astype(BF16)
        part = jnp.dot(a, wd[c0:c0 + part_w, :], preferred_element_type=F32)
        down = part if down is None else down + part
    o_ref[...] += down

    @pl.when(f == pl.num_programs(1) - 1)
    def _():
        y = x_ref[...] + FFN_RES * o_ref[...]
        if final:
            y = _rms(y, fg_ref[...])
        o_ref[...] = y


def _ffn(x, gain, wg, wu, wd, layer, final_gain=None, *, tm=512, tf=512, emit=False):
    m, d = x.shape
    f = wg.shape[-1]
    final = final_gain is not None
    fg = final_gain if final else gain[layer]
    if emit:
        assert m == tm, "every weight tile must be visited exactly once"
        wspec = lambda r, c, imap: pl.BlockSpec((None, r, c), lambda i, j: (layer,) + imap(j))
    else:
        wspec = lambda r, c, imap: pl.BlockSpec((r, c), lambda i, j: imap(j))
    up_map, down_map = (lambda j: (0, j)), (lambda j: (j, 0))
    once = dict(pipeline_mode=pl.Buffered(1)) if m == tm else {}
    out_specs = [pl.BlockSpec((tm, d), lambda i, j: (i, 0), **once)]
    out_shape = [jax.ShapeDtypeStruct((m, d), F32)]
    if emit:
        out_specs += [pl.BlockSpec((d, tf), lambda i, j: (0, j)), pl.BlockSpec((d, tf), lambda i, j: (0, j)),
                      pl.BlockSpec((tf, d), lambda i, j: (j, 0))]
        out_shape += [jax.ShapeDtypeStruct(s, BF16) for s in ((d, f), (d, f), (f, d))]
    out = pl.pallas_call(
        functools.partial(_ffn_kernel, final=final, emit=emit),
        grid=(m // tm, f // tf),
        in_specs=[
            pl.BlockSpec((tm, d), lambda i, j: (i, 0), **once),
            pl.BlockSpec((1, d), lambda i, j: (0, 0)),
            wspec(d, tf, up_map), wspec(d, tf, up_map), wspec(tf, d, down_map),
            pl.BlockSpec((1, d), lambda i, j: (0, 0)),
        ],
        out_specs=out_specs,
        out_shape=out_shape,
        scratch_shapes=[pltpu.VMEM((tm, d), BF16)],
        compiler_params=_params("parallel", "arbitrary"),
        name="ffn",
    )(x, gain[layer].reshape(1, d), wg, wu, wd, fg.reshape(1, d))
    return out if emit else out[0]


def _proj_kernel(*refs, norm, bias, pair, res, out_scale, n_dt, emit):
    refs = list(refs)
    x_ref = refs.pop(0)
    g_ref = refs.pop(0) if norm else None
    nb_ref = refs.pop(0) if norm == "ln_silu" else None
    n_w = 2 if pair else 1
    w_refs = [refs.pop(0) for _ in range(n_w)]
    b_refs = [refs.pop(0) for _ in range(n_w)] if bias else None
    r_ref = refs.pop(0) if res else None
    n_y = 2 if pair == "split" else 1
    o_refs = [[refs.pop(0) for _ in range(n_dt)] for _ in range(n_y)]
    w16_refs = [refs.pop(0) for _ in range(n_w)] if emit else None
    xn_ref = refs.pop(0) if norm else None

    if norm:
        @pl.when(pl.program_id(1) == 0)
        def _():
            if norm == "rms":
                _normalize_rows(x_ref, xn_ref, lambda x: _rms(x, g_ref[...]))
            else:
                _normalize_rows(x_ref, xn_ref, lambda x: _ln_silu(x, g_ref[...], nb_ref[...]))
        xn = xn_ref[...]
    else:
        xn = x_ref[...]

    ws = [w[...] for w in w_refs]
    if emit:
        ws = [w.astype(BF16) for w in ws]
        for w16_ref, w in zip(w16_refs, ws):
            w16_ref[...] = w
    ys = [jnp.dot(xn, w, preferred_element_type=F32) for w in ws]
    if bias:
        ys = [y + b[...] for y, b in zip(ys, b_refs)]
    if pair == "glu":
        ys = [ys[0] * jax.nn.sigmoid(ys[1])]
    if res:
        ys = [r_ref[...] + ys[0]]
    if out_scale is not None:
        ys = [y * out_scale for y in ys]
    for y, outs in zip(ys, o_refs):
        for o in outs:
            o[...] = y.astype(o.dtype)


def _proj(x, w, *, gain=None, ln_bias=None, bias=None, pair=None, residual=None, out_scale=None,
          out_dtypes=(F32,), tm=1024, tn=1024, emit=False):
    m, d = x.shape
    halves = w if isinstance(w, tuple) else ((w, w) if pair else (w,))
    n = halves[0].shape[1] // (1 if isinstance(w, tuple) or not pair else 2)
    tm, tn = min(tm, m), min(tn, n)
    nb = n // tn
    assert not emit or m == tm, "every weight tile must be visited exactly once"
    norm = None if gain is None else ("rms" if ln_bias is None else "ln_silu")
    once = dict(pipeline_mode=pl.Buffered(1)) if m == tm else {}
    args = [x]
    specs = [pl.BlockSpec((tm, d), lambda i, j: (i, 0), **once)]
    if norm:
        args.append(gain.reshape(1, d))
        specs.append(pl.BlockSpec((1, d), lambda i, j: (0, 0)))
    if norm == "ln_silu":
        args.append(ln_bias.reshape(1, d))
        specs.append(pl.BlockSpec((1, d), lambda i, j: (0, 0)))
    for k, half in enumerate(halves):
        shift = 0 if isinstance(w, tuple) else k * nb
        args.append(half)
        specs.append(pl.BlockSpec((d, tn), lambda i, j, shift=shift: (0, j + shift)))
    if bias is not None:
        b2 = bias.reshape(1, -1)
        args.append(b2)
        specs.append(pl.BlockSpec((1, tn), lambda i, j: (0, j)))
        if pair:
            args.append(b2)
            specs.append(pl.BlockSpec((1, tn), lambda i, j: (0, j + nb)))
    if residual is not None:
        args.append(residual)
        specs.append(pl.BlockSpec((tm, tn), lambda i, j: (i, j)))
    n_out = (2 if pair == "split" else 1) * len(out_dtypes)
    dts = list(out_dtypes) * (n_out // len(out_dtypes))
    out_specs = [pl.BlockSpec((tm, tn), lambda i, j: (i, j)) for _ in dts]
    out_shape = [jax.ShapeDtypeStruct((m, n), dt) for dt in dts]
    if emit:
        out_specs += [pl.BlockSpec((d, tn), lambda i, j: (0, j)) for _ in halves]
        out_shape += [jax.ShapeDtypeStruct((d, n), BF16) for _ in halves]
    out = pl.pallas_call(
        functools.partial(_proj_kernel, norm=norm, bias=bias is not None, pair=pair, res=residual is not None,
                          out_scale=out_scale, n_dt=len(out_dtypes), emit=emit),
        grid=(m // tm, nb),
        in_specs=specs,
        out_specs=out_specs,
        out_shape=out_shape,
        scratch_shapes=[pltpu.VMEM((tm, d), BF16)] if norm else [],
        compiler_params=_params("parallel", "arbitrary"),
        name="proj",
    )(*args)
    return out if len(out) > 1 else out[0]


def _proj_resident_kernel(*refs, norm, bias, pair, res, out_scale, n_dt, chunk):
    refs = list(refs)
    x_ref = refs.pop(0)
    g_ref = refs.pop(0) if norm else None
    nb_ref = refs.pop(0) if norm == "ln_silu" else None
    n_w = 2 if pair else 1
    w_refs = [refs.pop(0) for _ in range(n_w)]
    b_ref = refs.pop(0) if bias else None
    r_ref = refs.pop(0) if res else None
    o_refs = [refs[k * n_dt:(k + 1) * n_dt] for k in range(2 if pair == "split" else 1)]
    n = w_refs[0].shape[1]
    for r0 in range(0, x_ref.shape[0], chunk):
        rows = slice(r0, r0 + chunk)
        x = x_ref[rows, :]
        if norm == "rms":
            x = _rms(x, g_ref[...]).astype(BF16)
        elif norm == "ln_silu":
            x = _ln_silu(x, g_ref[...], nb_ref[...]).astype(BF16)
        ys = [jnp.dot(x, w[...], preferred_element_type=F32) for w in w_refs]
        if bias:
            ys = [y + b_ref[:, k * n:(k + 1) * n] for k, y in enumerate(ys)]
        if pair == "glu":
            ys = [ys[0] * jax.nn.sigmoid(ys[1])]
        if res:
            ys = [r_ref[rows, :] + ys[0]]
        if out_scale is not None:
            ys = [y * out_scale for y in ys]
        for y, outs in zip(ys, o_refs):
            for o in outs:
                o[rows, :] = y.astype(o.dtype)


def _proj_resident(x, w, *, gain=None, ln_bias=None, bias=None, pair=None, residual=None, out_scale=None,
                   out_dtypes=(F32,), tm=None, chunk=256):
    m, d = x.shape
    ws = w if pair else (w,)
    n = ws[0].shape[1]
    norm = None if gain is None else ("rms" if ln_bias is None else "ln_silu")
    dts = list(out_dtypes) * (2 if pair == "split" else 1)
    if tm is None:
        weight_bytes = sum(wk.size * wk.dtype.itemsize for wk in ws)
        row_bytes = d * x.dtype.itemsize + n * sum(jnp.dtype(dt).itemsize for dt in dts)
        row_bytes += 0 if residual is None else n * residual.dtype.itemsize
        tm = next(t for t in (512, 256) if weight_bytes + 2 * t * row_bytes <= RESIDENT_BUDGET)
    chunk = min(chunk, tm)
    vec = lambda v: (v.reshape(1, -1), pl.BlockSpec((1, v.size), lambda i: (0, 0)))
    args, specs = [x], [pl.BlockSpec((tm, d), lambda i: (i, 0))]
    for v in ([gain] if norm else []) + ([ln_bias] if norm == "ln_silu" else []):
        a, s = vec(v)
        args.append(a)
        specs.append(s)
    for wk in ws:
        args.append(wk)
        specs.append(pl.BlockSpec((d, n), lambda i: (0, 0), pipeline_mode=pl.Buffered(1)))
    if bias is not None:
        a, s = vec(bias)
        args.append(a)
        specs.append(s)
    if residual is not None:
        args.append(residual)
        specs.append(pl.BlockSpec((tm, n), lambda i: (i, 0)))
    out = pl.pallas_call(
        functools.partial(_proj_resident_kernel, norm=norm, bias=bias is not None, pair=pair,
                          res=residual is not None, out_scale=out_scale, n_dt=len(out_dtypes), chunk=chunk),
        grid=(m // tm,),
        in_specs=specs,
        out_specs=[pl.BlockSpec((tm, n), lambda i: (i, 0)) for _ in dts],
        out_shape=[jax.ShapeDtypeStruct((m, n), dt) for dt in dts],
        compiler_params=_params("parallel"),
        name="proj_resident",
    )(*args)
    return out if len(out) > 1 else out[0]


def _conv_kernel(u_ref, st_ref, dw_ref, db_ref, o_ref, ext_ref, *, tt, rows, lanes):
    t = pl.program_id(2)
    lo = HALO_PAD - HALO

    @pl.when(t == 0)
    def _():
        ext_ref[0:lo, :] = jnp.zeros((lo, ext_ref.shape[1]), F32)
        ext_ref[lo:HALO_PAD, :] = st_ref[0]

    @pl.when(t > 0)
    def _():
        ext_ref[0:HALO_PAD, :] = ext_ref[tt:tt + HALO_PAD, :]

    ext_ref[HALO_PAD:, :] = u_ref[0]

    taps = [[] for _ in range(SUBLANES)]
    for w in range(CONV_WIDTH):
        taps[(lo + w) % SUBLANES].append(((lo + w) // SUBLANES, w))
    for c0 in range(0, ext_ref.shape[1], lanes):
        cs = slice(c0, c0 + lanes)
        for r0 in range(0, tt, rows):
            acc = None
            for b in range(SUBLANES):
                n = rows + SUBLANES if b else rows
                part = None
                for a, w in taps[b]:
                    start = r0 + a * SUBLANES
                    term = ext_ref[start:start + n, cs] * dw_ref[w:w + 1, cs]
                    part = term if part is None else part + term
                part = part[b:b + rows]
                acc = part + db_ref[:, cs] if acc is None else acc + part
            o_ref[0, r0:r0 + rows, cs] = acc


def _conv(u, state, dw_w, dw_b, *, tt):
    b, t, d = u.shape
    cols = min(d, CONV_BLOCK_ELEMS // tt)
    return pl.pallas_call(
        functools.partial(_conv_kernel, tt=tt, rows=min(tt, 128), lanes=128),
        grid=(b, d // cols, t // tt),
        in_specs=[
            pl.BlockSpec((1, tt, cols), lambda i, c, j: (i, j, c)),
            pl.BlockSpec((1, HALO, cols), lambda i, c, j: (i, 0, c)),
            pl.BlockSpec((CONV_WIDTH, cols), lambda i, c, j: (0, c)),
            pl.BlockSpec((1, cols), lambda i, c, j: (0, c)),
        ],
        out_specs=pl.BlockSpec((1, tt, cols), lambda i, c, j: (i, j, c)),
        out_shape=jax.ShapeDtypeStruct((b, t, d), F32),
        scratch_shapes=[pltpu.VMEM((HALO_PAD + tt, cols), F32)],
        compiler_params=_params("parallel", "parallel", "arbitrary"),
        name="conv",
    )(u, state, dw_w, dw_b.reshape(1, d))


def _suffix_matrix(n):
    tri = lax.broadcasted_iota(jnp.int32, (n, n), 0) > lax.broadcasted_iota(jnp.int32, (n, n), 1)
    return jnp.concatenate([tri, tri], axis=0).astype(BF16)


def _attn_kernel(q_ref, kd_ref, vd_ref, kp_ref, vp_ref, *rest, g, tq, tk, causal_past, head_rows, n_far):
    if n_far:
        k_hbm, v_hbm, trid_ref, trip_ref, o_ref, acc_ref, mass_ref, kfar_ref, vfar_ref, far_sem = rest
    else:
        trid_ref, trip_ref, o_ref, acc_ref, mass_ref = rest
    nt = (((1,), (1,)), ((), ()))
    hd = HEAD_DIM
    qs = [q_ref[0, :, h * hd:(h + 1) * hd] for h in range(g)]

    def block(ks, vs, tri2, mask, first):
        zs, sps = [], []
        for h in range(g):
            z = lax.dot_general(qs[h], ks[h], nt, preferred_element_type=F32)
            neg_abs = lax.bitcast_convert_type(lax.bitcast_convert_type(z, jnp.int32) | SIGN_BIT, F32)
            sp = jnp.maximum(z, 0.0) + jnp.log(1.0 + jnp.exp2(neg_abs)) * LOG2E
            if mask is not None:
                sp = jnp.where(mask, sp, 0.0)
            zs.append(z)
            sps.append(sp)
        z = jnp.concatenate(zs, axis=0)
        sp = jnp.concatenate(sps, axis=0)
        rowsum = jnp.sum(sp, axis=1, keepdims=True)
        hi = sp.astype(BF16)
        lo = (sp - hi.astype(F32)).astype(BF16)
        n = sp.shape[1]
        if n % 128 == 0:
            suffix = jnp.dot(jnp.concatenate([hi, lo], axis=1), tri2, preferred_element_type=F32)
        else:
            suffix = (jnp.dot(hi, tri2[:n], preferred_element_type=F32)
                      + jnp.dot(lo, tri2[n:], preferred_element_type=F32))
        x = z - sp - suffix
        if not first:
            x = x - mass_ref[...]
        a = jnp.exp2(x)
        for h in range(g):
            ah = a[h * tq:(h + 1) * tq]
            if mask is not None:
                ah = jnp.where(mask, ah, 0.0)
            pv = jnp.dot(ah.astype(BF16), vs[h], preferred_element_type=F32)
            if first:
                acc_ref[:, h * hd:(h + 1) * hd] = pv
            else:
                acc_ref[:, h * hd:(h + 1) * hd] += pv
        if first:
            mass_ref[...] = rowsum
        else:
            mass_ref[...] += rowsum

    def diag_block():
        mask = lax.broadcasted_iota(jnp.int32, (tq, tq), 1) < lax.broadcasted_iota(jnp.int32, (tq, tq), 0)
        block([kd_ref[0, :, h * hd:(h + 1) * hd] for h in range(g)],
              [vd_ref[0, :, h * hd:(h + 1) * hd] for h in range(g)], trid_ref[...], mask, True)

    tri2 = trip_ref[...]
    if head_rows:
        n_near = kp_ref.shape[1] // (g * tk)

        def past_block(i):
            if i < n_near:
                kr, vr, start = kp_ref.at[0], vp_ref.at[0], (n_near - 1 - i) * tk * g
            else:
                src = pl.ds((n_far - 1 - (i - n_near)) * tk * g, tk * g)
                copies = [pltpu.make_async_copy(hbm.at[pl.program_id(0), src, :], buf, far_sem.at[j])
                          for j, (hbm, buf) in enumerate(((k_hbm, kfar_ref), (v_hbm, vfar_ref)))]
                for c in copies:
                    c.start()
                for c in copies:
                    c.wait()
                kr, vr, start = kfar_ref, vfar_ref, 0
            past = lambda ref, h: ref[pl.ds(start + h, tk, stride=g), :].astype(BF16)
            block([past(kr, h) for h in range(g)], [past(vr, h) for h in range(g)], tri2, None, False)

        diag_block()
        past_block(0)
        for i in range(1, n_near + n_far):
            pl.when(jnp.min(mass_ref[...]) < EXIT_LOG2)(functools.partial(past_block, i))
    else:
        n_past = pl.program_id(2) * (tq // tk) if causal_past else kp_ref.shape[1] // tk

        def past_block(i):
            off = pl.multiple_of((n_past - 1 - i) * tk, tk)
            past = lambda ref, h: ref[0, pl.ds(off, tk), h * hd:(h + 1) * hd].astype(BF16)
            block([past(kp_ref, h) for h in range(g)], [past(vp_ref, h) for h in range(g)], tri2, None, False)

        @pl.when(n_past == 0)
        def _():
            diag_block()

        @pl.when(n_past > 0)
        def _():
            diag_block()
            past_block(0)

        def cond(c):
            i, min_mass = c
            return jnp.logical_and(i < n_past, min_mass < EXIT_LOG2)

        def body(c):
            i, _ = c
            past_block(i)
            return i + 1, jnp.min(mass_ref[...])

        lax.while_loop(cond, body, (jnp.int32(1), jnp.min(mass_ref[...])))
    o_ref[0] = acc_ref[...].astype(o_ref.dtype)


def _attention(q, k_new, v_new, k_past, v_past, *, g, tq, tk, causal_past):
    b, t, _ = q.shape
    p = k_past.shape[1]
    gw = g * HEAD_DIM
    qspec = pl.BlockSpec((1, tq, gw), lambda bi, h, i: (bi, i, h))
    head_rows = k_past.ndim == 4
    past_args, past_specs, far_scratch, n_far = [k_past, v_past], [], [], 0
    if head_rows:
        assert g == N_HEADS and not causal_past
        k_past, v_past = (a.reshape(b, p * N_HEADS, HEAD_DIM) for a in (k_past, v_past))
        near = min(NEAR_BLOCKS * tk, p)
        n_far = (p - near) // tk
        near_spec = pl.BlockSpec((1, near * N_HEADS, HEAD_DIM), lambda bi, h, i: (bi, p // near - 1, 0))
        past_args, past_specs = [k_past, v_past], [near_spec, near_spec]
        if n_far:
            past_args += [k_past, v_past]
            past_specs += [pl.BlockSpec(memory_space=pl.ANY)] * 2
            far_scratch = [pltpu.VMEM((tk * N_HEADS, HEAD_DIM), k_past.dtype)] * 2 + [pltpu.SemaphoreType.DMA((2,))]
    else:
        past_specs = [pl.BlockSpec((1, p, gw), lambda bi, h, i: (bi, 0, h))] * 2
    tri_sizes = [tq, tk]
    tri_specs = [pl.BlockSpec((2 * n, n), lambda bi, h, i: (0, 0)) for n in tri_sizes]
    return pl.pallas_call(
        functools.partial(_attn_kernel, g=g, tq=tq, tk=tk, causal_past=causal_past, head_rows=head_rows,
                          n_far=n_far),
        grid=(b, N_HEADS // g, t // tq),
        in_specs=[qspec, qspec, qspec, *past_specs, *tri_specs],
        out_specs=qspec,
        out_shape=jax.ShapeDtypeStruct(q.shape, BF16),
        scratch_shapes=[pltpu.VMEM((tq, gw), F32), pltpu.VMEM((g * tq, 1), F32), *far_scratch],
        compiler_params=_params("parallel", "parallel", "arbitrary"),
        name="attention",
    )(q, k_new, v_new, *past_args, *[_suffix_matrix(n) for n in tri_sizes])


def _trunk(x, conv_state, cache_k, cache_v, p, mw, *, emit, conv_tt, attn_g, tq, tk):
    b, t, d = x.shape
    m = b * t
    hk = N_HEADS * HEAD_DIM
    x = x.reshape(m, d)
    w16 = {}

    def ffn(x, name, layer, final_gain=None):
        gain = p[name + "_norm"]
        if not emit:
            return _ffn(x, gain, *mw[name, layer], layer, final_gain)
        y, *w16[name, layer] = _ffn(x, gain, mw[name + "_w_gate"], mw[name + "_w_up"], mw[name + "_w_down"],
                                    layer, final_gain, tm=m, tf=256, emit=True)
        return y

    def proj(x, name, tn=1024, **kw):
        if not emit:
            return _proj_resident(x, mw[name], **kw)
        n_w = 2 if kw.get("pair") else 1
        *ys, = _proj(x, mw[name], tn=min(tn, 512 // n_w), emit=True, **kw)
        w16[name] = tuple(ys[-n_w:]) if n_w == 2 else ys[-1]
        ys = ys[:-n_w]
        return ys if len(ys) > 1 else ys[0]

    x = ffn(x, "ffn1", 0)
    u = proj(x, "conv_pw1_w", tn=512, gain=p["mix_norm"][0], bias=p["conv_pw1_b"][0], pair="glu")
    u = u.reshape(b, t, d)
    new_state = u[:, t - HALO:]
    c = _conv(u, conv_state, p["conv_dw_w"][0], p["conv_dw_b"][0], tt=conv_tt)
    x = proj(c.reshape(m, d), "conv_pw2_w", gain=p["conv_ln_g"][0], ln_bias=p["conv_ln_b"][0],
             bias=p["conv_pw2_b"][0], residual=x)
    x = ffn(x, "ffn2", 0)

    k_new, k16, v_new, v16 = proj(x, "w_kv", tn=512, gain=p["kv_norm"], pair="split", out_dtypes=(F32, BF16))

    x = ffn(x, "ffn1", 1)
    q = proj(x, "attn_wq", gain=p["mix_norm"][1], out_scale=Q_SCALE, out_dtypes=(BF16,))
    k_new, v_new, q = lax.optimization_barrier((k_new, v_new, q))
    q, k16, v16 = (a.reshape(b, t, hk) for a in (q, k16, v16))
    if cache_k is None:
        o = _attention(q, k16, v16, k16, v16, g=attn_g, tq=tq, tk=tk, causal_past=True)
    else:
        o = _attention(q, k16, v16, cache_k, cache_v, g=attn_g, tq=tq, tk=tk, causal_past=False)
    x = proj(o.reshape(m, hk), "attn_wo", residual=x)
    y = ffn(x, "ffn2", 1, p["final_norm"])
    return (y.reshape(b, t, d), new_state[None], k_new.reshape(b, t, N_HEADS, HEAD_DIM),
            v_new.reshape(b, t, N_HEADS, HEAD_DIM)), w16


def kernel(x_prompt, x_sample, state_conv, cache_k, cache_v, ffn1_norm, ffn1_w_gate, ffn1_w_up, ffn1_w_down, mix_norm, ffn2_norm, ffn2_w_gate, ffn2_w_up, ffn2_w_down, conv_pw1_w, conv_pw1_b, conv_dw_w, conv_dw_b, conv_ln_g, conv_ln_b, conv_pw2_w, conv_pw2_b, kv_norm, w_kv, attn_wq, attn_wo, final_norm):
    p = dict(
        ffn1_norm=ffn1_norm, mix_norm=mix_norm, ffn2_norm=ffn2_norm, kv_norm=kv_norm, final_norm=final_norm,
        conv_pw1_b=conv_pw1_b, conv_dw_w=conv_dw_w, conv_dw_b=conv_dw_b, conv_ln_g=conv_ln_g,
        conv_ln_b=conv_ln_b, conv_pw2_b=conv_pw2_b,
    )
    mw = dict(
        ffn1_w_gate=ffn1_w_gate, ffn1_w_up=ffn1_w_up, ffn1_w_down=ffn1_w_down,
        ffn2_w_gate=ffn2_w_gate, ffn2_w_up=ffn2_w_up, ffn2_w_down=ffn2_w_down,
        conv_pw1_w=conv_pw1_w[0], conv_pw2_w=conv_pw2_w[0], w_kv=w_kv, attn_wq=attn_wq[0], attn_wo=attn_wo[0],
    )
    bp, tp, _ = x_prompt.shape
    bs, ts, _ = x_sample.shape
    (y_s, st_s, k_s, v_s), w16 = _trunk(x_sample, state_conv[0], cache_k, cache_v, p, mw, emit=True,
                                        conv_tt=ts, attn_g=N_HEADS, tq=ts, tk=256)
    x_prompt, y_s = lax.optimization_barrier((x_prompt, y_s))
    zero_state = jnp.zeros((bp, HALO, D_MODEL), F32)
    (y_p, st_p, k_p, v_p), _ = _trunk(x_prompt, zero_state, None, None, p, w16, emit=False,
                                      conv_tt=512, attn_g=4, tq=256, tk=256)
    return (y_p, y_s, st_p, k_p, v_p, st_s, k_s, v_s)
```

```python
import functools

import jax
import jax.numpy as jnp
from jax import lax
from jax.experimental import pallas as pl
from jax.experimental.pallas import tpu as pltpu

D_MODEL = 2048
N_HEADS = 16
HEAD_DIM = 128
CONV_WIDTH = 31
EPS = 1e-6
FFN_RES = 0.5
LOG2E = 1.4426950408889634
Q_SCALE = HEAD_DIM ** -0.5 * LOG2E
EXIT_LOG2 = 150.0
SIGN_BIT = -2 ** 31
NEAR_BLOCKS = 2

SUBLANES = 8
MXU_COLS = 256
FFN_TILE = 512
HALO = CONV_WIDTH - 1
HALO_PAD = 32
CONV_BLOCK_ELEMS = 256 * 1024

VMEM_LIMIT = 56 * 1024 * 1024
RESIDENT_BUDGET = VMEM_LIMIT - 20 * 1024 * 1024

F32 = jnp.float32
BF16 = jnp.bfloat16


def _params(*sem):
    return pltpu.CompilerParams(dimension_semantics=sem, vmem_limit_bytes=VMEM_LIMIT)


def _rms(x, g):
    return x * lax.rsqrt(jnp.mean(x * x, axis=-1, keepdims=True) + EPS) * g


def _ln_silu(c, g, b):
    mu = jnp.mean(c, axis=-1, keepdims=True)
    xc = c - mu
    y = xc * lax.rsqrt(jnp.mean(xc * xc, axis=-1, keepdims=True) + EPS) * g + b
    return y * jax.nn.sigmoid(y)


def _normalize_rows(x_ref, xn_ref, fn, chunk=256):
    chunk = min(chunk, x_ref.shape[0])

    def body(r, carry):
        rows = pl.ds(pl.multiple_of(r * chunk, chunk), chunk)
        xn_ref[rows, :] = fn(x_ref[rows, :]).astype(BF16)
        return carry

    lax.fori_loop(0, x_ref.shape[0] // chunk, body, 0)


def _ffn_kernel(x_ref, g_ref, wg_ref, wu_ref, wd_ref, fg_ref, o_ref, *rest, final, emit):
    w16_refs, xn_ref = rest[:-1], rest[-1]
    f = pl.program_id(1)

    @pl.when(f == 0)
    def _():
        _normalize_rows(x_ref, xn_ref, lambda x: _rms(x, g_ref[...]))
        o_ref[...] = jnp.zeros_like(o_ref)

    ws = [w[...] for w in (wg_ref, wu_ref, wd_ref)]
    if emit:
        ws = [w.astype(BF16) for w in ws]
        for w16_ref, w in zip(w16_refs, ws):
            w16_ref[...] = w
    wg, wu, wd = ws
    xn = xn_ref[...]
    down = None
    tf = wg.shape[1]
    part_w = tf // 2 if tf // 2 >= MXU_COLS else tf
    for c0 in range(0, tf, part_w):
        h = jnp.dot(xn, wg[:, c0:c0 + part_w], preferred_element_type=F32)
        u = jnp.dot(xn, wu[:, c0:c0 + part_w], preferred_element_type=F32)
        a = (h * jax.nn.sigmoid(h) * u).astype(BF16)
        part = jnp.dot(a, wd[c0:c0 + part_w, :], preferred_element_type=F32)
        down = part if down is None else down + part
    o_ref[...] += down

    @pl.when(f == pl.num_programs(1) - 1)
    def _():
        y = x_ref[...] + FFN_RES * o_ref[...]
        if final:
            y = _rms(y, fg_ref[...])
        o_ref[...] = y


def _ffn(x, gain, wg, wu, wd, layer, final_gain=None, *, tm=512, tf=FFN_TILE, emit=False):
    m, d = x.shape
    final = final_gain is not None
    fg = final_gain if final else gain[layer]
    once = dict(pipeline_mode=pl.Buffered(1)) if m == tm else {}
    out_specs = [pl.BlockSpec((tm, d), lambda i, j: (i, 0), **once)]
    out_shape = [jax.ShapeDtypeStruct((m, d), F32)]
    if emit:
        assert m == tm, "every weight tile must be visited exactly once"
        f = wg.shape[-1]
        per = FFN_TILE // tf
        w_specs = [pl.BlockSpec((None, d, tf), lambda i, j: (layer, 0, j))] * 2
        w_specs += [pl.BlockSpec((None, tf, d), lambda i, j: (layer, j, 0))]
        out_specs += [pl.BlockSpec((None, d, tf), lambda i, j: (j // per, 0, j % per))] * 2
        out_specs += [pl.BlockSpec((tf, d), lambda i, j: (j, 0))]
        out_shape += [jax.ShapeDtypeStruct(s, BF16) for s in ((f // FFN_TILE, d, FFN_TILE),) * 2 + ((f, d),)]
    else:
        assert tf == FFN_TILE == wg.shape[2]
        f = wd.shape[0]
        w_specs = [pl.BlockSpec((None, d, tf), lambda i, j: (j, 0, 0))] * 2
        w_specs += [pl.BlockSpec((tf, d), lambda i, j: (j, 0))]
    out = pl.pallas_call(
        functools.partial(_ffn_kernel, final=final, emit=emit),
        grid=(m // tm, f // tf),
        in_specs=[
            pl.BlockSpec((tm, d), lambda i, j: (i, 0), **once),
            pl.BlockSpec((1, d), lambda i, j: (0, 0)),
            *w_specs,
            pl.BlockSpec((1, d), lambda i, j: (0, 0)),
        ],
        out_specs=out_specs,
        out_shape=out_shape,
        scratch_shapes=[pltpu.VMEM((tm, d), BF16)],
        compiler_params=_params("parallel", "arbitrary"),
        name="ffn",
    )(x, gain[layer].reshape(1, d), wg, wu, wd, fg.reshape(1, d))
    return out if emit else out[0]


def _proj_kernel(*refs, norm, bias, pair, res, out_scale, n_dt, emit):
    refs = list(refs)
    x_ref = refs.pop(0)
    g_ref = refs.pop(0) if norm else None
    nb_ref = refs.pop(0) if norm == "ln_silu" else None
    n_w = 2 if pair else 1
    w_refs = [refs.pop(0) for _ in range(n_w)]
    b_refs = [refs.pop(0) for _ in range(n_w)] if bias else None
    r_ref = refs.pop(0) if res else None
    n_y = 2 if pair == "split" else 1
    o_refs = [[refs.pop(0) for _ in range(n_dt)] for _ in range(n_y)]
    w16_refs = [refs.pop(0) for _ in range(n_w)] if emit else None
    xn_ref = refs.pop(0) if norm else None

    if norm:
        @pl.when(pl.program_id(1) == 0)
        def _():
            if norm == "rms":
                _normalize_rows(x_ref, xn_ref, lambda x: _rms(x, g_ref[...]))
            else:
                _normalize_rows(x_ref, xn_ref, lambda x: _ln_silu(x, g_ref[...], nb_ref[...]))
        xn = xn_ref[...]
    else:
        xn = x_ref[...]

    ws = [w[...] for w in w_refs]
    if emit:
        ws = [w.astype(BF16) for w in ws]
        for w16_ref, w in zip(w16_refs, ws):
            w16_ref[...] = w
    ys = [jnp.dot(xn, w, preferred_element_type=F32) for w in ws]
    if bias:
        ys = [y + b[...] for y, b in zip(ys, b_refs)]
    if pair == "glu":
        ys = [ys[0] * jax.nn.sigmoid(ys[1])]
    if res:
        ys = [r_ref[...] + ys[0]]
    if out_scale is not None:
        ys = [y * out_scale for y in ys]
    for y, outs in zip(ys, o_refs):
        for o in outs:
            o[...] = y.astype(o.dtype)


def _proj(x, w, *, gain=None, ln_bias=None, bias=None, pair=None, residual=None, out_scale=None,
          out_dtypes=(F32,), tm=1024, tn=1024, emit=False):
    m, d = x.shape
    halves = w if isinstance(w, tuple) else ((w, w) if pair else (w,))
    n = halves[0].shape[1] // (1 if isinstance(w, tuple) or not pair else 2)
    tm, tn = min(tm, m), min(tn, n)
    nb = n // tn
    assert not emit or m == tm, "every weight tile must be visited exactly once"
    norm = None if gain is None else ("rms" if ln_bias is None else "ln_silu")
    once = dict(pipeline_mode=pl.Buffered(1)) if m == tm else {}
    args = [x]
    specs = [pl.BlockSpec((tm, d), lambda i, j: (i, 0), **once)]
    if norm:
        args.append(gain.reshape(1, d))
        specs.append(pl.BlockSpec((1, d), lambda i, j: (0, 0)))
    if norm == "ln_silu":
        args.append(ln_bias.reshape(1, d))
        specs.append(pl.BlockSpec((1, d), lambda i, j: (0, 0)))
    for k, half in enumerate(halves):
        shift = 0 if isinstance(w, tuple) else k * nb
        args.append(half)
        specs.append(pl.BlockSpec((d, tn), lambda i, j, shift=shift: (0, j + shift)))
    if bias is not None:
        b2 = bias.reshape(1, -1)
        args.append(b2)
        specs.append(pl.BlockSpec((1, tn), lambda i, j: (0, j)))
        if pair:
            args.append(b2)
            specs.append(pl.BlockSpec((1, tn), lambda i, j: (0, j + nb)))
    if residual is not None:
        args.append(residual)
        specs.append(pl.BlockSpec((tm, tn), lambda i, j: (i, j)))
    n_out = (2 if pair == "split" else 1) * len(out_dtypes)
    dts = list(out_dtypes) * (n_out // len(out_dtypes))
    out_specs = [pl.BlockSpec((tm, tn), lambda i, j: (i, j)) for _ in dts]
    out_shape = [jax.ShapeDtypeStruct((m, n), dt) for dt in dts]
    if emit:
        out_specs += [pl.BlockSpec((d, tn), lambda i, j: (0, j)) for _ in halves]
        out_shape += [jax.ShapeDtypeStruct((d, n), BF16) for _ in halves]
    out = pl.pallas_call(
        functools.partial(_proj_kernel, norm=norm, bias=bias is not None, pair=pair, res=residual is not None,
                          out_scale=out_scale, n_dt=len(out_dtypes), emit=emit),
        grid=(m // tm, nb),
        in_specs=specs,
        out_specs=out_specs,
        out_shape=out_shape,
        scratch_shapes=[pltpu.VMEM((tm, d), BF16)] if norm else [],
        compiler_params=_params("parallel", "arbitrary"),
        name="proj",
    )(*args)
    return out if len(out) > 1 else out[0]


def _proj_resident_kernel(*refs, norm, bias, pair, res, out_scale, n_dt, chunk):
    refs = list(refs)
    x_ref = refs.pop(0)
    g_ref = refs.pop(0) if norm else None
    nb_ref = refs.pop(0) if norm == "ln_silu" else None
    n_w = 2 if pair else 1
    w_refs = [refs.pop(0) for _ in range(n_w)]
    b_ref = refs.pop(0) if bias else None
    r_ref = refs.pop(0) if res else None
    o_refs = [refs[k * n_dt:(k + 1) * n_dt] for k in range(2 if pair == "split" else 1)]
    n = w_refs[0].shape[1]
    for r0 in range(0, x_ref.shape[0], chunk):
        rows = slice(r0, r0 + chunk)
        x = x_ref[rows, :]
        if norm == "rms":
            x = _rms(x, g_ref[...]).astype(BF16)
        elif norm == "ln_silu":
            x = _ln_silu(x, g_ref[...], nb_ref[...]).astype(BF16)
        ys = [jnp.dot(x, w[...], preferred_element_type=F32) for w in w_refs]
        if bias:
            ys = [y + b_ref[:, k * n:(k + 1) * n] for k, y in enumerate(ys)]
        if pair == "glu":
            ys = [ys[0] * jax.nn.sigmoid(ys[1])]
        if res:
            ys = [r_ref[rows, :] + ys[0]]
        if out_scale is not None:
            ys = [y * out_scale for y in ys]
        for y, outs in zip(ys, o_refs):
            for o in outs:
                o[rows, :] = y.astype(o.dtype)


def _proj_resident(x, w, *, gain=None, ln_bias=None, bias=None, pair=None, residual=None, out_scale=None,
                   out_dtypes=(F32,), tm=None, chunk=256):
    m, d = x.shape
    ws = w if pair else (w,)
    n = ws[0].shape[1]
    norm = None if gain is None else ("rms" if ln_bias is None else "ln_silu")
    dts = list(out_dtypes) * (2 if pair == "split" else 1)
    if tm is None:
        weight_bytes = sum(wk.size * wk.dtype.itemsize for wk in ws)
        row_bytes = d * x.dtype.itemsize + n * sum(jnp.dtype(dt).itemsize for dt in dts)
        row_bytes += 0 if residual is None else n * residual.dtype.itemsize
        tm = next(t for t in (512, 256) if weight_bytes + 2 * t * row_bytes <= RESIDENT_BUDGET)
    chunk = min(chunk, tm)
    vec = lambda v: (v.reshape(1, -1), pl.BlockSpec((1, v.size), lambda i: (0, 0)))
    args, specs = [x], [pl.BlockSpec((tm, d), lambda i: (i, 0))]
    for v in ([gain] if norm else []) + ([ln_bias] if norm == "ln_silu" else []):
        a, s = vec(v)
        args.append(a)
        specs.append(s)
    for wk in ws:
        args.append(wk)
        specs.append(pl.BlockSpec((d, n), lambda i: (0, 0), pipeline_mode=pl.Buffered(1)))
    if bias is not None:
        a, s = vec(bias)
        args.append(a)
        specs.append(s)
    if residual is not None:
        args.append(residual)
        specs.append(pl.BlockSpec((tm, n), lambda i: (i, 0)))
    out = pl.pallas_call(
        functools.partial(_proj_resident_kernel, norm=norm, bias=bias is not None, pair=pair,
                          res=residual is not None, out_scale=out_scale, n_dt=len(out_dtypes), chunk=chunk),
        grid=(m // tm,),
        in_specs=specs,
        out_specs=[pl.BlockSpec((tm, n), lambda i: (i, 0)) for _ in dts],
        out_shape=[jax.ShapeDtypeStruct((m, n), dt) for dt in dts],
        compiler_params=_params("parallel"),
        name="proj_resident",
    )(*args)
    return out if len(out) > 1 else out[0]


def _conv_kernel(u_ref, st_ref, dw_ref, db_ref, o_ref, ext_ref, *, tt, rows, lanes):
    t = pl.program_id(2)
    lo = HALO_PAD - HALO

    @pl.when(t == 0)
    def _():
        ext_ref[0:lo, :] = jnp.zeros((lo, ext_ref.shape[1]), F32)
        ext_ref[lo:HALO_PAD, :] = st_ref[0]

    @pl.when(t > 0)
    def _():
        ext_ref[0:HALO_PAD, :] = ext_ref[tt:tt + HALO_PAD, :]

    ext_ref[HALO_PAD:, :] = u_ref[0]

    taps = [[] for _ in range(SUBLANES)]
    for w in range(CONV_WIDTH):
        taps[(lo + w) % SUBLANES].append(((lo + w) // SUBLANES, w))
    for c0 in range(0, ext_ref.shape[1], lanes):
        cs = slice(c0, c0 + lanes)
        for r0 in range(0, tt, rows):
            acc = None
            for b in range(SUBLANES):
                n = rows + SUBLANES if b else rows
                part = None
                for a, w in taps[b]:
                    start = r0 + a * SUBLANES
                    term = ext_ref[start:start + n, cs] * dw_ref[w:w + 1, cs]
                    part = term if part is None else part + term
                part = part[b:b + rows]
                acc = part + db_ref[:, cs] if acc is None else acc + part
            o_ref[0, r0:r0 + rows, cs] = acc


def _conv(u, state, dw_w, dw_b, *, tt):
    b, t, d = u.shape
    cols = min(d, CONV_BLOCK_ELEMS // tt)
    return pl.pallas_call(
        functools.partial(_conv_kernel, tt=tt, rows=min(tt, 128), lanes=128),
        grid=(b, d // cols, t // tt),
        in_specs=[
            pl.BlockSpec((1, tt, cols), lambda i, c, j: (i, j, c)),
            pl.BlockSpec((1, HALO, cols), lambda i, c, j: (i, 0, c)),
            pl.BlockSpec((CONV_WIDTH, cols), lambda i, c, j: (0, c)),
            pl.BlockSpec((1, cols), lambda i, c, j: (0, c)),
        ],
        out_specs=pl.BlockSpec((1, tt, cols), lambda i, c, j: (i, j, c)),
        out_shape=jax.ShapeDtypeStruct((b, t, d), F32),
        scratch_shapes=[pltpu.VMEM((HALO_PAD + tt, cols), F32)],
        compiler_params=_params("parallel", "parallel", "arbitrary"),
        name="conv",
    )(u, state, dw_w, dw_b.reshape(1, d))


def _suffix_matrix(n):
    tri = lax.broadcasted_iota(jnp.int32, (n, n), 0) > lax.broadcasted_iota(jnp.int32, (n, n), 1)
    return jnp.concatenate([tri, tri], axis=0).astype(BF16)


def _attn_kernel(q_ref, kd_ref, vd_ref, kp_ref, vp_ref, *rest, g, tq, tk, causal_past, head_rows, n_far):
    if n_far:
        k_hbm, v_hbm, trid_ref, trip_ref, o_ref, acc_ref, mass_ref, kfar_ref, vfar_ref, far_sem = rest
    else:
        trid_ref, trip_ref, o_ref, acc_ref, mass_ref = rest
    nt = (((1,), (1,)), ((), ()))
    hd = HEAD_DIM
    qs = [q_ref[0, :, h * hd:(h + 1) * hd] for h in range(g)]

    def block(ks, vs, tri2, mask, first):
        zs, sps = [], []
        for h in range(g):
            z = lax.dot_general(qs[h], ks[h], nt, preferred_element_type=F32)
            neg_abs = lax.bitcast_convert_type(lax.bitcast_convert_type(z, jnp.int32) | SIGN_BIT, F32)
            sp = jnp.maximum(z, 0.0) + jnp.log(1.0 + jnp.exp2(neg_abs)) * LOG2E
            if mask is not None:
                sp = jnp.where(mask, sp, 0.0)
            zs.append(z)
            sps.append(sp)
        z = jnp.concatenate(zs, axis=0)
        sp = jnp.concatenate(sps, axis=0)
        rowsum = jnp.sum(sp, axis=1, keepdims=True)
        hi = sp.astype(BF16)
        lo = (sp - hi.astype(F32)).astype(BF16)
        n = sp.shape[1]
        if n % 128 == 0:
            suffix = jnp.dot(jnp.concatenate([hi, lo], axis=1), tri2, preferred_element_type=F32)
        else:
            suffix = (jnp.dot(hi, tri2[:n], preferred_element_type=F32)
                      + jnp.dot(lo, tri2[n:], preferred_element_type=F32))
        x = z - sp - suffix
        if not first:
            x = x - mass_ref[...]
        a = jnp.exp2(x)
        for h in range(g):
            ah = a[h * tq:(h + 1) * tq]
            if mask is not None:
                ah = jnp.where(mask, ah, 0.0)
            pv = jnp.dot(ah.astype(BF16), vs[h], preferred_element_type=F32)
            if first:
                acc_ref[:, h * hd:(h + 1) * hd] = pv
            else:
                acc_ref[:, h * hd:(h + 1) * hd] += pv
        if first:
            mass_ref[...] = rowsum
        else:
            mass_ref[...] += rowsum

    def diag_block():
        mask = lax.broadcasted_iota(jnp.int32, (tq, tq), 1) < lax.broadcasted_iota(jnp.int32, (tq, tq), 0)
        block([kd_ref[0, :, h * hd:(h + 1) * hd] for h in range(g)],
              [vd_ref[0, :, h * hd:(h + 1) * hd] for h in range(g)], trid_ref[...], mask, True)

    tri2 = trip_ref[...]
    if head_rows:
        n_near = kp_ref.shape[1] // (g * tk)

        def past_block(i):
            if i < n_near:
                kr, vr, start = kp_ref.at[0], vp_ref.at[0], (n_near - 1 - i) * tk * g
            else:
                src = pl.ds((n_far - 1 - (i - n_near)) * tk * g, tk * g)
                copies = [pltpu.make_async_copy(hbm.at[pl.program_id(0), src, :], buf, far_sem.at[j])
                          for j, (hbm, buf) in enumerate(((k_hbm, kfar_ref), (v_hbm, vfar_ref)))]
                for c in copies:
                    c.start()
                for c in copies:
                    c.wait()
                kr, vr, start = kfar_ref, vfar_ref, 0
            past = lambda ref, h: ref[pl.ds(start + h, tk, stride=g), :].astype(BF16)
            block([past(kr, h) for h in range(g)], [past(vr, h) for h in range(g)], tri2, None, False)

        diag_block()
        past_block(0)
        for i in range(1, n_near + n_far):
            pl.when(jnp.min(mass_ref[...]) < EXIT_LOG2)(functools.partial(past_block, i))
    else:
        n_past = pl.program_id(2) * (tq // tk) if causal_past else kp_ref.shape[1] // tk

        def past_block(i):
            off = pl.multiple_of((n_past - 1 - i) * tk, tk)
            past = lambda ref, h: ref[0, pl.ds(off, tk), h * hd:(h + 1) * hd].astype(BF16)
            block([past(kp_ref, h) for h in range(g)], [past(vp_ref, h) for h in range(g)], tri2, None, False)

        @pl.when(n_past == 0)
        def _():
            diag_block()

        @pl.when(n_past > 0)
        def _():
            diag_block()
            past_block(0)

        def cond(c):
            i, min_mass = c
            return jnp.logical_and(i < n_past, min_mass < EXIT_LOG2)

        def body(c):
            i, _ = c
            past_block(i)
            return i + 1, jnp.min(mass_ref[...])

        lax.while_loop(cond, body, (jnp.int32(1), jnp.min(mass_ref[...])))
    o_ref[0] = acc_ref[...].astype(o_ref.dtype)


def _attention(q, k_new, v_new, k_past, v_past, *, g, tq, tk, causal_past):
    b, t, _ = q.shape
    p = k_past.shape[1]
    gw = g * HEAD_DIM
    qspec = pl.BlockSpec((1, tq, gw), lambda bi, h, i: (bi, i, h))
    head_rows = k_past.ndim == 4
    past_args, past_specs, far_scratch, n_far = [k_past, v_past], [], [], 0
    if head_rows:
        assert g == N_HEADS and not causal_past
        k_past, v_past = (a.reshape(b, p * N_HEADS, HEAD_DIM) for a in (k_past, v_past))
        near = min(NEAR_BLOCKS * tk, p)
        n_far = (p - near) // tk
        near_spec = pl.BlockSpec((1, near * N_HEADS, HEAD_DIM), lambda bi, h, i: (bi, p // near - 1, 0))
        past_args, past_specs = [k_past, v_past], [near_spec, near_spec]
        if n_far:
            past_args += [k_past, v_past]
            past_specs += [pl.BlockSpec(memory_space=pl.ANY)] * 2
            far_scratch = [pltpu.VMEM((tk * N_HEADS, HEAD_DIM), k_past.dtype)] * 2 + [pltpu.SemaphoreType.DMA((2,))]
    else:
        past_specs = [pl.BlockSpec((1, p, gw), lambda bi, h, i: (bi, 0, h))] * 2
    tri_sizes = [tq, tk]
    tri_specs = [pl.BlockSpec((2 * n, n), lambda bi, h, i: (0, 0)) for n in tri_sizes]
    return pl.pallas_call(
        functools.partial(_attn_kernel, g=g, tq=tq, tk=tk, causal_past=causal_past, head_rows=head_rows,
                          n_far=n_far),
        grid=(b, N_HEADS // g, t // tq),
        in_specs=[qspec, qspec, qspec, *past_specs, *tri_specs],
        out_specs=qspec,
        out_shape=jax.ShapeDtypeStruct(q.shape, BF16),
        scratch_shapes=[pltpu.VMEM((tq, gw), F32), pltpu.VMEM((g * tq, 1), F32), *far_scratch],
        compiler_params=_params("parallel", "parallel", "arbitrary"),
        name="attention",
    )(q, k_new, v_new, *past_args, *[_suffix_matrix(n) for n in tri_sizes])


def _trunk(x, conv_state, cache_k, cache_v, p, mw, *, emit, conv_tt, attn_g, tq, tk):
    b, t, d = x.shape
    m = b * t
    hk = N_HEADS * HEAD_DIM
    x = x.reshape(m, d)
    w16 = {}

    def ffn(x, name, layer, final_gain=None):
        gain = p[name + "_norm"]
        if not emit:
            return _ffn(x, gain, *mw[name, layer], layer, final_gain)
        y, *w16[name, layer] = _ffn(x, gain, mw[name + "_w_gate"], mw[name + "_w_up"], mw[name + "_w_down"],
                                    layer, final_gain, tm=m, tf=256, emit=True)
        return y

    def proj(x, name, tn=1024, **kw):
        if not emit:
            return _proj_resident(x, mw[name], **kw)
        n_w = 2 if kw.get("pair") else 1
        *ys, = _proj(x, mw[name], tn=min(tn, 512 // n_w), emit=True, **kw)
        w16[name] = tuple(ys[-n_w:]) if n_w == 2 else ys[-1]
        ys = ys[:-n_w]
        return ys if len(ys) > 1 else ys[0]

    x = ffn(x, "ffn1", 0)
    u = proj(x, "conv_pw1_w", tn=512, gain=p["mix_norm"][0], bias=p["conv_pw1_b"][0], pair="glu")
    u = u.reshape(b, t, d)
    new_state = u[:, t - HALO:]
    c = _conv(u, conv_state, p["conv_dw_w"][0], p["conv_dw_b"][0], tt=conv_tt)
    x = proj(c.reshape(m, d), "conv_pw2_w", gain=p["conv_ln_g"][0], ln_bias=p["conv_ln_b"][0],
             bias=p["conv_pw2_b"][0], residual=x)
    x = ffn(x, "ffn2", 0)

    k_new, k16, v_new, v16 = proj(x, "w_kv", tn=512, gain=p["kv_norm"], pair="split", out_dtypes=(F32, BF16))

    x = ffn(x, "ffn1", 1)
    q = proj(x, "attn_wq", gain=p["mix_norm"][1], out_scale=Q_SCALE, out_dtypes=(BF16,))
    k_new, v_new, q = lax.optimization_barrier((k_new, v_new, q))
    q, k16, v16 = (a.reshape(b, t, hk) for a in (q, k16, v16))
    if cache_k is None:
        o = _attention(q, k16, v16, k16, v16, g=attn_g, tq=tq, tk=tk, causal_past=True)
    else:
        o = _attention(q, k16, v16, cache_k, cache_v, g=attn_g, tq=tq, tk=tk, causal_past=False)
    x = proj(o.reshape(m, hk), "attn_wo", residual=x)
    y = ffn(x, "ffn2", 1, p["final_norm"])
    return (y.reshape(b, t, d), new_state[None], k_new.reshape(b, t, N_HEADS, HEAD_DIM),
            v_new.reshape(b, t, N_HEADS, HEAD_DIM)), w16


def kernel(x_prompt, x_sample, state_conv, cache_k, cache_v, ffn1_norm, ffn1_w_gate, ffn1_w_up, ffn1_w_down, mix_norm, ffn2_norm, ffn2_w_gate, ffn2_w_up, ffn2_w_down, conv_pw1_w, conv_pw1_b, conv_dw_w, conv_dw_b, conv_ln_g, conv_ln_b, conv_pw2_w, conv_pw2_b, kv_norm, w_kv, attn_wq, attn_wo, final_norm):
    p = dict(
        ffn1_norm=ffn1_norm, mix_norm=mix_norm, ffn2_norm=ffn2_norm, kv_norm=kv_norm, final_norm=final_norm,
        conv_pw1_b=conv_pw1_b, conv_dw_w=conv_dw_w, conv_dw_b=conv_dw_b, conv_ln_g=conv_ln_g,
        conv_ln_b=conv_ln_b, conv_pw2_b=conv_pw2_b,
    )
    mw = dict(
        ffn1_w_gate=ffn1_w_gate, ffn1_w_up=ffn1_w_up, ffn1_w_down=ffn1_w_down,
        ffn2_w_gate=ffn2_w_gate, ffn2_w_up=ffn2_w_up, ffn2_w_down=ffn2_w_down,
        conv_pw1_w=conv_pw1_w[0], conv_pw2_w=conv_pw2_w[0], w_kv=w_kv, attn_wq=attn_wq[0], attn_wo=attn_wo[0],
    )
    bp, tp, _ = x_prompt.shape
    bs, ts, _ = x_sample.shape
    (y_s, st_s, k_s, v_s), w16 = _trunk(x_sample, state_conv[0], cache_k, cache_v, p, mw, emit=True,
                                        conv_tt=ts, attn_g=N_HEADS, tq=ts, tk=256)
    x_prompt, y_s = lax.optimization_barrier((x_prompt, y_s))
    zero_state = jnp.zeros((bp, HALO, D_MODEL), F32)
    (y_p, st_p, k_p, v_p), _ = _trunk(x_prompt, zero_state, None, None, p, w16, emit=False,
                                      conv_tt=512, attn_g=4, tq=256, tk=256)
    return (y_p, y_s, st_p, k_p, v_p, st_s, k_s, v_s)
```

```python
import functools

import jax
import jax.numpy as jnp
from jax import lax
from jax.experimental import pallas as pl
from jax.experimental.pallas import tpu as pltpu

D_MODEL = 2048
N_HEADS = 16
HEAD_DIM = 128
CONV_WIDTH = 31
EPS = 1e-6
FFN_RES = 0.5
LOG2E = 1.4426950408889634
Q_SCALE = HEAD_DIM ** -0.5 * LOG2E
EXIT_LOG2 = 150.0
SIGN_BIT = -2 ** 31
NEAR_BLOCKS = 2

SUBLANES = 8
MXU_COLS = 256
FFN_TILE = 512
HALO = CONV_WIDTH - 1
HALO_PAD = 32
CONV_BLOCK_ELEMS = 256 * 1024

VMEM_LIMIT = 56 * 1024 * 1024
RESIDENT_BUDGET = VMEM_LIMIT - 16 * 1024 * 1024

F32 = jnp.float32
BF16 = jnp.bfloat16


def _params(*sem):
    return pltpu.CompilerParams(dimension_semantics=sem, vmem_limit_bytes=VMEM_LIMIT)


def _rms(x, g):
    return x * lax.rsqrt(jnp.mean(x * x, axis=-1, keepdims=True) + EPS) * g


def _ln_silu(c, g, b):
    mu = jnp.mean(c, axis=-1, keepdims=True)
    xc = c - mu
    y = xc * lax.rsqrt(jnp.mean(xc * xc, axis=-1, keepdims=True) + EPS) * g + b
    return y * jax.nn.sigmoid(y)


def _normalize_rows(x_ref, xn_ref, fn, chunk=256):
    chunk = min(chunk, x_ref.shape[0])

    def body(r, carry):
        rows = pl.ds(pl.multiple_of(r * chunk, chunk), chunk)
        xn_ref[rows, :] = fn(x_ref[rows, :]).astype(BF16)
        return carry

    lax.fori_loop(0, x_ref.shape[0] // chunk, body, 0)


def _ffn_kernel(x_ref, g_ref, wg_ref, wu_ref, wd_ref, fg_ref, *rest, final, emit, normed):
    rest = list(rest)
    xn_ref = rest.pop(0) if normed else rest.pop()
    o_ref, w16_refs = rest[0], rest[1:]
    f = pl.program_id(1)

    @pl.when(f == 0)
    def _():
        if not normed:
            _normalize_rows(x_ref, xn_ref, lambda x: _rms(x, g_ref[...]))
        o_ref[...] = jnp.zeros_like(o_ref)

    ws = [w[...] for w in (wg_ref, wu_ref, wd_ref)]
    if emit:
        ws = [w.astype(BF16) for w in ws]
        for w16_ref, w in zip(w16_refs, ws):
            w16_ref[...] = w
    wg, wu, wd = ws
    xn = xn_ref[...]
    down = None
    tf = wg.shape[1]
    part_w = tf // 2 if tf // 2 >= MXU_COLS else tf
    for c0 in range(0, tf, part_w):
        h = jnp.dot(xn, wg[:, c0:c0 + part_w], preferred_element_type=F32)
        u = jnp.dot(xn, wu[:, c0:c0 + part_w], preferred_element_type=F32)
        a = (h * jax.nn.sigmoid(h) * u).astype(BF16)
        part = jnp.dot(a, wd[c0:c0 + part_w, :], preferred_element_type=F32)
        down = part if down is None else down + part
    o_ref[...] += down

    @pl.when(f == pl.num_programs(1) - 1)
    def _():
        y = x_ref[...] + FFN_RES * o_ref[...]
        if final:
            y = _rms(y, fg_ref[...])
        o_ref[...] = y


def _ffn(x, gain, wg, wu, wd, layer, final_gain=None, *, xn=None, tm=512, tf=FFN_TILE, emit=False):
    m, d = x.shape
    final = final_gain is not None
    fg = final_gain if final else gain[layer]
    once = dict(pipeline_mode=pl.Buffered(1)) if m == tm else {}
    out_specs = [pl.BlockSpec((tm, d), lambda i, j: (i, 0), **once)]
    out_shape = [jax.ShapeDtypeStruct((m, d), F32)]
    if emit:
        assert m == tm, "every weight tile must be visited exactly once"
        f = wg.shape[-1]
        per = FFN_TILE // tf
        w_specs = [pl.BlockSpec((None, d, tf), lambda i, j: (layer, 0, j))] * 2
        w_specs += [pl.BlockSpec((None, tf, d), lambda i, j: (layer, j, 0))]
        out_specs += [pl.BlockSpec((None, d, tf), lambda i, j: (j // per, 0, j % per))] * 2
        out_specs += [pl.BlockSpec((tf, d), lambda i, j: (j, 0))]
        out_shape += [jax.ShapeDtypeStruct(s, BF16) for s in ((f // FFN_TILE, d, FFN_TILE),) * 2 + ((f, d),)]
    else:
        assert tf == FFN_TILE == wg.shape[2]
        f = wd.shape[0]
        w_specs = [pl.BlockSpec((None, d, tf), lambda i, j: (j, 0, 0))] * 2
        w_specs += [pl.BlockSpec((tf, d), lambda i, j: (j, 0))]
    normed = xn is not None
    out = pl.pallas_call(
        functools.partial(_ffn_kernel, final=final, emit=emit, normed=normed),
        grid=(m // tm, f // tf),
        in_specs=[
            pl.BlockSpec((tm, d), lambda i, j: (i, 0), **once),
            pl.BlockSpec((1, d), lambda i, j: (0, 0)),
            *w_specs,
            pl.BlockSpec((1, d), lambda i, j: (0, 0)),
            *([pl.BlockSpec((tm, d), lambda i, j: (i, 0))] if normed else []),
        ],
        out_specs=out_specs,
        out_shape=out_shape,
        scratch_shapes=[] if normed else [pltpu.VMEM((tm, d), BF16)],
        compiler_params=_params("parallel", "arbitrary"),
        name="ffn",
    )(x, gain[layer].reshape(1, d), wg, wu, wd, fg.reshape(1, d), *([xn] if normed else []))
    return out if emit else out[0]


def _proj_kernel(*refs, norm, bias, pair, res, out_scale, n_dt, emit):
    refs = list(refs)
    x_ref = refs.pop(0)
    g_ref = refs.pop(0) if norm else None
    nb_ref = refs.pop(0) if norm == "ln_silu" else None
    n_w = 2 if pair else 1
    w_refs = [refs.pop(0) for _ in range(n_w)]
    b_refs = [refs.pop(0) for _ in range(n_w)] if bias else None
    r_ref = refs.pop(0) if res else None
    n_y = 2 if pair == "split" else 1
    o_refs = [[refs.pop(0) for _ in range(n_dt)] for _ in range(n_y)]
    w16_refs = [refs.pop(0) for _ in range(n_w)] if emit else None
    xn_ref = refs.pop(0) if norm else None

    if norm:
        @pl.when(pl.program_id(1) == 0)
        def _():
            if norm == "rms":
                _normalize_rows(x_ref, xn_ref, lambda x: _rms(x, g_ref[...]))
            else:
                _normalize_rows(x_ref, xn_ref, lambda x: _ln_silu(x, g_ref[...], nb_ref[...]))
        xn = xn_ref[...]
    else:
        xn = x_ref[...]

    ws = [w[...] for w in w_refs]
    if emit:
        ws = [w.astype(BF16) for w in ws]
        for w16_ref, w in zip(w16_refs, ws):
            w16_ref[...] = w
    ys = [jnp.dot(xn, w, preferred_element_type=F32) for w in ws]
    if bias:
        ys = [y + b[...] for y, b in zip(ys, b_refs)]
    if pair == "glu":
        ys = [ys[0] * jax.nn.sigmoid(ys[1])]
    if res:
        ys = [r_ref[...] + ys[0]]
    if out_scale is not None:
        ys = [y * out_scale for y in ys]
    for y, outs in zip(ys, o_refs):
        for o in outs:
            o[...] = y.astype(o.dtype)


def _proj(x, w, *, gain=None, ln_bias=None, bias=None, pair=None, residual=None, out_scale=None,
          out_dtypes=(F32,), tm=1024, tn=1024, emit=False):
    m, d = x.shape
    halves = w if isinstance(w, tuple) else ((w, w) if pair else (w,))
    n = halves[0].shape[1] // (1 if isinstance(w, tuple) or not pair else 2)
    tm, tn = min(tm, m), min(tn, n)
    nb = n // tn
    assert not emit or m == tm, "every weight tile must be visited exactly once"
    norm = None if gain is None else ("rms" if ln_bias is None else "ln_silu")
    once = dict(pipeline_mode=pl.Buffered(1)) if m == tm else {}
    args = [x]
    specs = [pl.BlockSpec((tm, d), lambda i, j: (i, 0), **once)]
    if norm:
        args.append(gain.reshape(1, d))
        specs.append(pl.BlockSpec((1, d), lambda i, j: (0, 0)))
    if norm == "ln_silu":
        args.append(ln_bias.reshape(1, d))
        specs.append(pl.BlockSpec((1, d), lambda i, j: (0, 0)))
    for k, half in enumerate(halves):
        shift = 0 if isinstance(w, tuple) else k * nb
        args.append(half)
        specs.append(pl.BlockSpec((d, tn), lambda i, j, shift=shift: (0, j + shift)))
    if bias is not None:
        b2 = bias.reshape(1, -1)
        args.append(b2)
        specs.append(pl.BlockSpec((1, tn), lambda i, j: (0, j)))
        if pair:
            args.append(b2)
            specs.append(pl.BlockSpec((1, tn), lambda i, j: (0, j + nb)))
    if residual is not None:
        args.append(residual)
        specs.append(pl.BlockSpec((tm, tn), lambda i, j: (i, j)))
    n_out = (2 if pair == "split" else 1) * len(out_dtypes)
    dts = list(out_dtypes) * (n_out // len(out_dtypes))
    out_specs = [pl.BlockSpec((tm, tn), lambda i, j: (i, j)) for _ in dts]
    out_shape = [jax.ShapeDtypeStruct((m, n), dt) for dt in dts]
    if emit:
        out_specs += [pl.BlockSpec((d, tn), lambda i, j: (0, j)) for _ in halves]
        out_shape += [jax.ShapeDtypeStruct((d, n), BF16) for _ in halves]
    out = pl.pallas_call(
        functools.partial(_proj_kernel, norm=norm, bias=bias is not None, pair=pair, res=residual is not None,
                          out_scale=out_scale, n_dt=len(out_dtypes), emit=emit),
        grid=(m // tm, nb),
        in_specs=specs,
        out_specs=out_specs,
        out_shape=out_shape,
        scratch_shapes=[pltpu.VMEM((tm, d), BF16)] if norm else [],
        compiler_params=_params("parallel", "arbitrary"),
        name="proj",
    )(*args)
    return out if len(out) > 1 else out[0]


def _proj_resident_kernel(*refs, norm, bias, pair, res, out_scale, n_dt, chunk, also):
    refs = list(refs)
    x_ref = refs.pop(0)
    g_ref = refs.pop(0) if norm else None
    nb_ref = refs.pop(0) if norm == "ln_silu" else None
    n_w = 2 if pair else 1
    w_refs = [refs.pop(0) for _ in range(n_w)]
    b_ref = refs.pop(0) if bias else None
    r_ref = refs.pop(0) if res else None
    g2_ref = refs.pop(0) if also else None
    n2_ref = refs.pop() if also else None
    o_refs = [refs[k * n_dt:(k + 1) * n_dt] for k in range(2 if pair == "split" else 1)]
    n = w_refs[0].shape[1]
    for r0 in range(0, x_ref.shape[0], chunk):
        rows = slice(r0, r0 + chunk)
        x = x_ref[rows, :]
        if also:
            n2_ref[rows, :] = _rms(x, g2_ref[...]).astype(BF16)
        if norm == "rms":
            x = _rms(x, g_ref[...]).astype(BF16)
        elif norm == "ln_silu":
            x = _ln_silu(x, g_ref[...], nb_ref[...]).astype(BF16)
        ys = [jnp.dot(x, w[...], preferred_element_type=F32) for w in w_refs]
        if bias:
            ys = [y + b_ref[:, k * n:(k + 1) * n] for k, y in enumerate(ys)]
        if pair == "glu":
            ys = [ys[0] * jax.nn.sigmoid(ys[1])]
        if res:
            ys = [r_ref[rows, :] + ys[0]]
        if out_scale is not None:
            ys = [y * out_scale for y in ys]
        for y, outs in zip(ys, o_refs):
            for o in outs:
                o[rows, :] = y.astype(o.dtype)


def _proj_resident(x, w, *, gain=None, ln_bias=None, bias=None, pair=None, residual=None, out_scale=None,
                   out_dtypes=(F32,), next_gain=None, tm=None, chunk=256):
    m, d = x.shape
    ws = w if pair else (w,)
    n = ws[0].shape[1]
    norm = None if gain is None else ("rms" if ln_bias is None else "ln_silu")
    dts = list(out_dtypes) * (2 if pair == "split" else 1)
    if tm is None:
        weight_bytes = sum(wk.size * wk.dtype.itemsize for wk in ws)
        row_bytes = d * x.dtype.itemsize + n * sum(jnp.dtype(dt).itemsize for dt in dts)
        row_bytes += 0 if residual is None else n * residual.dtype.itemsize
        row_bytes += 0 if next_gain is None else next_gain.size * jnp.dtype(BF16).itemsize
        tm = next(t for t in (512, 256) if weight_bytes + 2 * t * row_bytes <= RESIDENT_BUDGET)
    chunk = min(chunk, tm)
    vec = lambda v: (v.reshape(1, -1), pl.BlockSpec((1, v.size), lambda i: (0, 0)))
    args, specs = [x], [pl.BlockSpec((tm, d), lambda i: (i, 0))]
    for v in ([gain] if norm else []) + ([ln_bias] if norm == "ln_silu" else []):
        a, s = vec(v)
        args.append(a)
        specs.append(s)
    for wk in ws:
        args.append(wk)
        specs.append(pl.BlockSpec((d, n), lambda i: (0, 0), pipeline_mode=pl.Buffered(1)))
    if bias is not None:
        a, s = vec(bias)
        args.append(a)
        specs.append(s)
    if residual is not None:
        args.append(residual)
        specs.append(pl.BlockSpec((tm, n), lambda i: (i, 0)))
    also = next_gain is not None
    out_specs = [pl.BlockSpec((tm, n), lambda i: (i, 0)) for _ in dts]
    out_shape = [jax.ShapeDtypeStruct((m, n), dt) for dt in dts]
    if also:
        assert x.dtype == F32 and next_gain.size == d
        a, s = vec(next_gain)
        args.append(a)
        specs.append(s)
        out_specs.append(pl.BlockSpec((tm, d), lambda i: (i, 0)))
        out_shape.append(jax.ShapeDtypeStruct((m, d), BF16))
    out = pl.pallas_call(
        functools.partial(_proj_resident_kernel, norm=norm, bias=bias is not None, pair=pair,
                          res=residual is not None, out_scale=out_scale, n_dt=len(out_dtypes), chunk=chunk,
                          also=also),
        grid=(m // tm,),
        in_specs=specs,
        out_specs=out_specs,
        out_shape=out_shape,
        compiler_params=_params("parallel"),
        name="proj_resident",
    )(*args)
    return out if len(out) > 1 else out[0]


def _conv_kernel(u_ref, st_ref, dw_ref, db_ref, o_ref, ext_ref, *, tt, rows, lanes):
    t = pl.program_id(2)
    lo = HALO_PAD - HALO

    @pl.when(t == 0)
    def _():
        ext_ref[0:lo, :] = jnp.zeros((lo, ext_ref.shape[1]), F32)
        ext_ref[lo:HALO_PAD, :] = st_ref[0]

    @pl.when(t > 0)
    def _():
        ext_ref[0:HALO_PAD, :] = ext_ref[tt:tt + HALO_PAD, :]

    ext_ref[HALO_PAD:, :] = u_ref[0]

    taps = [[] for _ in range(SUBLANES)]
    for w in range(CONV_WIDTH):
        taps[(lo + w) % SUBLANES].append(((lo + w) // SUBLANES, w))
    for c0 in range(0, ext_ref.shape[1], lanes):
        cs = slice(c0, c0 + lanes)
        for r0 in range(0, tt, rows):
            acc = None
            for b in range(SUBLANES):
                n = rows + SUBLANES if b else rows
                part = None
                for a, w in taps[b]:
                    start = r0 + a * SUBLANES
                    term = ext_ref[start:start + n, cs] * dw_ref[w:w + 1, cs]
                    part = term if part is None else part + term
                part = part[b:b + rows]
                acc = part + db_ref[:, cs] if acc is None else acc + part
            o_ref[0, r0:r0 + rows, cs] = acc


def _conv(u, state, dw_w, dw_b, *, tt):
    b, t, d = u.shape
    cols = min(d, CONV_BLOCK_ELEMS // tt)
    return pl.pallas_call(
        functools.partial(_conv_kernel, tt=tt, rows=min(tt, 128), lanes=128),
        grid=(b, d // cols, t // tt),
        in_specs=[
            pl.BlockSpec((1, tt, cols), lambda i, c, j: (i, j, c)),
            pl.BlockSpec((1, HALO, cols), lambda i, c, j: (i, 0, c)),
            pl.BlockSpec((CONV_WIDTH, cols), lambda i, c, j: (0, c)),
            pl.BlockSpec((1, cols), lambda i, c, j: (0, c)),
        ],
        out_specs=pl.BlockSpec((1, tt, cols), lambda i, c, j: (i, j, c)),
        out_shape=jax.ShapeDtypeStruct((b, t, d), F32),
        scratch_shapes=[pltpu.VMEM((HALO_PAD + tt, cols), F32)],
        compiler_params=_params("parallel", "parallel", "arbitrary"),
        name="conv",
    )(u, state, dw_w, dw_b.reshape(1, d))


def _suffix_matrix(n):
    tri = lax.broadcasted_iota(jnp.int32, (n, n), 0) > lax.broadcasted_iota(jnp.int32, (n, n), 1)
    return jnp.concatenate([tri, tri], axis=0).astype(BF16)


def _attn_kernel(q_ref, kd_ref, vd_ref, kp_ref, vp_ref, *rest, g, tq, tk, causal_past, head_rows, n_far):
    if n_far:
        k_hbm, v_hbm, trid_ref, trip_ref, o_ref, acc_ref, mass_ref, kfar_ref, vfar_ref, far_sem = rest
    else:
        trid_ref, trip_ref, o_ref, acc_ref, mass_ref = rest
    nt = (((1,), (1,)), ((), ()))
    hd = HEAD_DIM
    qs = [q_ref[0, :, h * hd:(h + 1) * hd] for h in range(g)]

    def block(ks, vs, tri2, mask, first):
        zs, sps = [], []
        for h in range(g):
            z = lax.dot_general(qs[h], ks[h], nt, preferred_element_type=F32)
            neg_abs = lax.bitcast_convert_type(lax.bitcast_convert_type(z, jnp.int32) | SIGN_BIT, F32)
            sp = jnp.maximum(z, 0.0) + jnp.log(1.0 + jnp.exp2(neg_abs)) * LOG2E
            if mask is not None:
                sp = jnp.where(mask, sp, 0.0)
            zs.append(z)
            sps.append(sp)
        z = jnp.concatenate(zs, axis=0)
        sp = jnp.concatenate(sps, axis=0)
        rowsum = jnp.sum(sp, axis=1, keepdims=True)
        hi = sp.astype(BF16)
        lo = (sp - hi.astype(F32)).astype(BF16)
        n = sp.shape[1]
        if n % 128 == 0:
            suffix = jnp.dot(jnp.concatenate([hi, lo], axis=1), tri2, preferred_element_type=F32)
        else:
            suffix = (jnp.dot(hi, tri2[:n], preferred_element_type=F32)
                      + jnp.dot(lo, tri2[n:], preferred_element_type=F32))
        x = z - sp - suffix
        if not first:
            x = x - mass_ref[...]
        a = jnp.exp2(x)
        for h in range(g):
            ah = a[h * tq:(h + 1) * tq]
            if mask is not None:
                ah = jnp.where(mask, ah, 0.0)
            pv = jnp.dot(ah.astype(BF16), vs[h], preferred_element_type=F32)
            if first:
                acc_ref[:, h * hd:(h + 1) * hd] = pv
            else:
                acc_ref[:, h * hd:(h + 1) * hd] += pv
        if first:
            mass_ref[...] = rowsum
        else:
            mass_ref[...] += rowsum

    def diag_block():
        mask = lax.broadcasted_iota(jnp.int32, (tq, tq), 1) < lax.broadcasted_iota(jnp.int32, (tq, tq), 0)
        block([kd_ref[0, :, h * hd:(h + 1) * hd] for h in range(g)],
              [vd_ref[0, :, h * hd:(h + 1) * hd] for h in range(g)], trid_ref[...], mask, True)

    tri2 = trip_ref[...]
    if head_rows:
        n_near = kp_ref.shape[1] // (g * tk)

        def past_block(i):
            if i < n_near:
                kr, vr, start = kp_ref.at[0], vp_ref.at[0], (n_near - 1 - i) * tk * g
            else:
                src = pl.ds((n_far - 1 - (i - n_near)) * tk * g, tk * g)
                copies = [pltpu.make_async_copy(hbm.at[pl.program_id(0), src, :], buf, far_sem.at[j])
                          for j, (hbm, buf) in enumerate(((k_hbm, kfar_ref), (v_hbm, vfar_ref)))]
                for c in copies:
                    c.start()
                for c in copies:
                    c.wait()
                kr, vr, start = kfar_ref, vfar_ref, 0
            past = lambda ref, h: ref[pl.ds(start + h, tk, stride=g), :].astype(BF16)
            block([past(kr, h) for h in range(g)], [past(vr, h) for h in range(g)], tri2, None, False)

        diag_block()
        past_block(0)
        for i in range(1, n_near + n_far):
            pl.when(jnp.min(mass_ref[...]) < EXIT_LOG2)(functools.partial(past_block, i))
    else:
        n_past = pl.program_id(2) * (tq // tk) if causal_past else kp_ref.shape[1] // tk

        def past_block(i):
            off = pl.multiple_of((n_past - 1 - i) * tk, tk)
            past = lambda ref, h: ref[0, pl.ds(off, tk), h * hd:(h + 1) * hd].astype(BF16)
            block([past(kp_ref, h) for h in range(g)], [past(vp_ref, h) for h in range(g)], tri2, None, False)

        @pl.when(n_past == 0)
        def _():
            diag_block()

        @pl.when(n_past > 0)
        def _():
            diag_block()
            past_block(0)

        def cond(c):
            i, min_mass = c
            return jnp.logical_and(i < n_past, min_mass < EXIT_LOG2)

        def body(c):
            i, _ = c
            past_block(i)
            return i + 1, jnp.min(mass_ref[...])

        lax.while_loop(cond, body, (jnp.int32(1), jnp.min(mass_ref[...])))
    o_ref[0] = acc_ref[...].astype(o_ref.dtype)


def _attention(q, k_new, v_new, k_past, v_past, *, g, tq, tk, causal_past):
    b, t, _ = q.shape
    p = k_past.shape[1]
    gw = g * HEAD_DIM
    qspec = pl.BlockSpec((1, tq, gw), lambda bi, h, i: (bi, i, h))
    head_rows = k_past.ndim == 4
    past_args, past_specs, far_scratch, n_far = [k_past, v_past], [], [], 0
    if head_rows:
        assert g == N_HEADS and not causal_past
        k_past, v_past = (a.reshape(b, p * N_HEADS, HEAD_DIM) for a in (k_past, v_past))
        near = min(NEAR_BLOCKS * tk, p)
        n_far = (p - near) // tk
        near_spec = pl.BlockSpec((1, near * N_HEADS, HEAD_DIM), lambda bi, h, i: (bi, p // near - 1, 0))
        past_args, past_specs = [k_past, v_past], [near_spec, near_spec]
        if n_far:
            past_args += [k_past, v_past]
            past_specs += [pl.BlockSpec(memory_space=pl.ANY)] * 2
            far_scratch = [pltpu.VMEM((tk * N_HEADS, HEAD_DIM), k_past.dtype)] * 2 + [pltpu.SemaphoreType.DMA((2,))]
    else:
        past_specs = [pl.BlockSpec((1, p, gw), lambda bi, h, i: (bi, 0, h))] * 2
    tri_sizes = [tq, tk]
    tri_specs = [pl.BlockSpec((2 * n, n), lambda bi, h, i: (0, 0)) for n in tri_sizes]
    return pl.pallas_call(
        functools.partial(_attn_kernel, g=g, tq=tq, tk=tk, causal_past=causal_past, head_rows=head_rows,
                          n_far=n_far),
        grid=(b, N_HEADS // g, t // tq),
        in_specs=[qspec, qspec, qspec, *past_specs, *tri_specs],
        out_specs=qspec,
        out_shape=jax.ShapeDtypeStruct(q.shape, BF16),
        scratch_shapes=[pltpu.VMEM((tq, gw), F32), pltpu.VMEM((g * tq, 1), F32), *far_scratch],
        compiler_params=_params("parallel", "parallel", "arbitrary"),
        name="attention",
    )(q, k_new, v_new, *past_args, *[_suffix_matrix(n) for n in tri_sizes])


def _trunk(x, conv_state, cache_k, cache_v, p, mw, *, emit, conv_tt, attn_g, tq, tk):
    b, t, d = x.shape
    m = b * t
    hk = N_HEADS * HEAD_DIM
    x = x.reshape(m, d)
    w16 = {}

    def ffn(x, name, layer, final_gain=None, xn=None):
        gain = p[name + "_norm"]
        if not emit:
            return _ffn(x, gain, *mw[name, layer], layer, final_gain, xn=xn)
        y, *w16[name, layer] = _ffn(x, gain, mw[name + "_w_gate"], mw[name + "_w_up"], mw[name + "_w_down"],
                                    layer, final_gain, tm=m, tf=256, emit=True)
        return y

    def proj(x, name, tn=1024, next_ffn=None, **kw):
        if not emit:
            if next_ffn is not None:
                kw["next_gain"] = p[next_ffn[0] + "_norm"][next_ffn[1]]
            return _proj_resident(x, mw[name], **kw)
        n_w = 2 if kw.get("pair") else 1
        *ys, = _proj(x, mw[name], tn=min(tn, 512 // n_w), emit=True, **kw)
        w16[name] = tuple(ys[-n_w:]) if n_w == 2 else ys[-1]
        ys = ys[:-n_w] + ([None] if next_ffn is not None else [])
        return ys if len(ys) > 1 else ys[0]

    x = ffn(x, "ffn1", 0)
    u = proj(x, "conv_pw1_w", tn=512, gain=p["mix_norm"][0], bias=p["conv_pw1_b"][0], pair="glu")
    u = u.reshape(b, t, d)
    new_state = u[:, t - HALO:]
    c = _conv(u, conv_state, p["conv_dw_w"][0], p["conv_dw_b"][0], tt=conv_tt)
    x = proj(c.reshape(m, d), "conv_pw2_w", gain=p["conv_ln_g"][0], ln_bias=p["conv_ln_b"][0],
             bias=p["conv_pw2_b"][0], residual=x)
    x = ffn(x, "ffn2", 0)

    k_new, k16, v_new, v16, xn = proj(x, "w_kv", tn=512, next_ffn=("ffn1", 1), gain=p["kv_norm"], pair="split",
                                      out_dtypes=(F32, BF16))

    x = ffn(x, "ffn1", 1, xn=xn)
    q = proj(x, "attn_wq", gain=p["mix_norm"][1], out_scale=Q_SCALE, out_dtypes=(BF16,))
    k_new, v_new, q = lax.optimization_barrier((k_new, v_new, q))
    q, k16, v16 = (a.reshape(b, t, hk) for a in (q, k16, v16))
    if cache_k is None:
        o = _attention(q, k16, v16, k16, v16, g=attn_g, tq=tq, tk=tk, causal_past=True)
    else:
        o = _attention(q, k16, v16, cache_k, cache_v, g=attn_g, tq=tq, tk=tk, causal_past=False)
    x = proj(o.reshape(m, hk), "attn_wo", residual=x)
    y = ffn(x, "ffn2", 1, p["final_norm"])
    return (y.reshape(b, t, d), new_state[None], k_new.reshape(b, t, N_HEADS, HEAD_DIM),
            v_new.reshape(b, t, N_HEADS, HEAD_DIM)), w16


def kernel(x_prompt, x_sample, state_conv, cache_k, cache_v, ffn1_norm, ffn1_w_gate, ffn1_w_up, ffn1_w_down, mix_norm, ffn2_norm, ffn2_w_gate, ffn2_w_up, ffn2_w_down, conv_pw1_w, conv_pw1_b, conv_dw_w, conv_dw_b, conv_ln_g, conv_ln_b, conv_pw2_w, conv_pw2_b, kv_norm, w_kv, attn_wq, attn_wo, final_norm):
    p = dict(
        ffn1_norm=ffn1_norm, mix_norm=mix_norm, ffn2_norm=ffn2_norm, kv_norm=kv_norm, final_norm=final_norm,
        conv_pw1_b=conv_pw1_b, conv_dw_w=conv_dw_w, conv_dw_b=conv_dw_b, conv_ln_g=conv_ln_g,
        conv_ln_b=conv_ln_b, conv_pw2_b=conv_pw2_b,
    )
    mw = dict(
        ffn1_w_gate=ffn1_w_gate, ffn1_w_up=ffn1_w_up, ffn1_w_down=ffn1_w_down,
        ffn2_w_gate=ffn2_w_gate, ffn2_w_up=ffn2_w_up, ffn2_w_down=ffn2_w_down,
        conv_pw1_w=conv_pw1_w[0], conv_pw2_w=conv_pw2_w[0], w_kv=w_kv, attn_wq=attn_wq[0], attn_wo=attn_wo[0],
    )
    bp, tp, _ = x_prompt.shape
    bs, ts, _ = x_sample.shape
    (y_s, st_s, k_s, v_s), w16 = _trunk(x_sample, state_conv[0], cache_k, cache_v, p, mw, emit=True,
                                        conv_tt=ts, attn_g=N_HEADS, tq=ts, tk=256)
    x_prompt, y_s = lax.optimization_barrier((x_prompt, y_s))
    zero_state = jnp.zeros((bp, HALO, D_MODEL), F32)
    (y_p, st_p, k_p, v_p), _ = _trunk(x_prompt, zero_state, None, None, p, w16, emit=False,
                                      conv_tt=512, attn_g=4, tq=256, tk=256)
    return (y_p, y_s, st_p, k_p, v_p, st_s, k_s, v_s)
```
